```python
import math
import jax, jax.numpy as jnp
from jax import lax
import numpy as np

D_MODEL = 1024
BATCH = 2
SEQ = 8192
DEPTH = 4

D_MIX = D_MODEL
D_SSM = D_MIX // 2
SSM_GROUP = 16
N_SSM_GROUPS = D_SSM // SSM_GROUP
SSM_STATE = 64
D_GMLP = D_MIX - D_SSM
GMLP_HEADS = 8
GMLP_HEAD_DIM = D_GMLP // GMLP_HEADS
CHUNK = 128
MEM_LEN = 256
XATTN_HEADS = 4
XATTN_HEAD_DIM = D_MODEL // XATTN_HEADS
D_FF = 2816
CONV_WIDTH = 3
RMS_EPS = 1e-6
DT_MIN = 1e-3
DT_MAX = 1e-1

kernel_name = "hybrid_s5_gmlp_xattn_convffn"


def rmsnorm(x, g):
    xf = x.astype(jnp.float32)
    xf = xf * lax.rsqrt(jnp.mean(xf * xf, axis=-1, keepdims=True) + RMS_EPS)
    return xf.astype(x.dtype) * g


def _ssm_combine(left, right):
    a1r, a1i, b1r, b1i = left
    a2r, a2i, b2r, b2i = right
    ar = a2r * a1r - a2i * a1i
    ai = a2r * a1i + a2i * a1r
    br = a2r * b1r - a2i * b1i + b2r
    bi = a2r * b1i + a2i * b1r + b2i
    return ar, ai, br, bi


def s5_ssm(u, lam_re, lam_im, log_dt, b_re, b_im, c_re, c_im, d_skip):
    bsz, L, _ = u.shape
    f32 = jnp.float32
    uf = u.astype(f32).reshape(bsz, L, N_SSM_GROUPS, SSM_GROUP)
    lr, li = lam_re.astype(f32), lam_im.astype(f32)
    dt = jnp.exp(log_dt.astype(f32))[:, None]
    mag = jnp.exp(lr * dt)
    ab_r, ab_i = mag * jnp.cos(li * dt), mag * jnp.sin(li * dt)
    nr, ni = ab_r - 1.0, ab_i
    den = lr * lr + li * li
    fr = (nr * lr + ni * li) / den
    fi = (ni * lr - nr * li) / den
    br, bi = b_re.astype(f32), b_im.astype(f32)
    bb_r = fr[..., None] * br - fi[..., None] * bi
    bb_i = fr[..., None] * bi + fi[..., None] * br
    drive_r = jnp.einsum('blgc,gpc->blgp', uf, bb_r)
    drive_i = jnp.einsum('blgc,gpc->blgp', uf, bb_i)
    a_r = jnp.broadcast_to(ab_r, drive_r.shape)
    a_i = jnp.broadcast_to(ab_i, drive_i.shape)
    _, _, xr, xi = lax.associative_scan(_ssm_combine, (a_r, a_i, drive_r, drive_i), axis=1)
    y = (jnp.einsum('blgp,gcp->blgc', xr, c_re.astype(f32))
         - jnp.einsum('blgp,gcp->blgc', xi, c_im.astype(f32))
         + d_skip.astype(f32) * uf)
    return y.reshape(bsz, L, D_SSM).astype(u.dtype)


def chunked_spatial_gate(u, v, g_v, w_s, b_s):
    bsz, L, _ = u.shape
    nc = L // CHUNK
    vh = v.reshape(bsz, nc, CHUNK, GMLP_HEADS, GMLP_HEAD_DIM)
    vh = rmsnorm(vh, g_v.reshape(GMLP_HEADS, GMLP_HEAD_DIM))
    mask = jnp.tril(jnp.ones((CHUNK, CHUNK), dtype=w_s.dtype))
    mixed = jnp.einsum('hts,bnshd->bnthd', w_s * mask, vh) + b_s.T[None, None, :, :, None]
    return u * mixed.reshape(bsz, L, D_GMLP)


def memory_cross_attention(h, mem_n, w_q, w_k, w_v, w_o):
    bsz, L, _ = h.shape
    q = (h @ w_q).reshape(bsz, L, XATTN_HEADS, XATTN_HEAD_DIM)
    k = (mem_n @ w_k).reshape(bsz, -1, XATTN_HEADS, XATTN_HEAD_DIM)
    v = (mem_n @ w_v).reshape(bsz, -1, XATTN_HEADS, XATTN_HEAD_DIM)
    s = jnp.einsum('blhd,bmhd->bhlm', q, k).astype(jnp.float32) / math.sqrt(XATTN_HEAD_DIM)
    p = jax.nn.softmax(s, axis=-1).astype(v.dtype)
    o = jnp.einsum('bhlm,bmhd->blhd', p, v).reshape(bsz, L, D_MODEL)
    return o @ w_o


def causal_depthwise_conv(x, w, b):
    c = x.shape[-1]
    y = lax.conv_general_dilated(x, w[:, None, :], window_strides=(1,),
                                 padding=[(CONV_WIDTH - 1, 0)],
                                 dimension_numbers=('NWC', 'WIO', 'NWC'),
                                 feature_group_count=c)
    return y + b


def conv_ffn(h, w_up, conv_w, conv_b, w_down):
    a = causal_depthwise_conv(h @ w_up, conv_w, conv_b)
    val, gate = jnp.split(a, 2, axis=-1)
    return (val * jax.nn.gelu(gate)) @ w_down


def setup_inputs(seed: int = 0) -> dict:
    key = jax.random.key(seed)
    ks = jax.random.split(key, 32)
    f32 = jnp.float32
    nrm = lambda k, shape, scale: jax.random.normal(k, shape, f32) * scale
    gain = lambda k, n: 1.0 + 0.01 * jax.random.normal(k, (DEPTH, n), f32)
    G, P, C = N_SSM_GROUPS, SSM_STATE, SSM_GROUP
    lam_im_init = math.pi * jnp.arange(P, dtype=f32)
    return {
        "x": nrm(ks[0], (BATCH, SEQ, D_MODEL), 1.0),
        "mem": nrm(ks[1], (BATCH, MEM_LEN, D_MODEL), 1.0),
        "g_mix_pre": gain(ks[2], D_MODEL),
        "w_in": nrm(ks[3], (DEPTH, D_MODEL, 2 * D_SSM + 2 * D_GMLP), D_MODEL ** -0.5),
        "lam_re": -0.5 + 0.01 * jax.random.normal(ks[4], (DEPTH, G, P), f32),
        "lam_im": lam_im_init + 0.01 * jax.random.normal(ks[5], (DEPTH, G, P), f32),
        "log_dt": jax.random.uniform(ks[6], (DEPTH, G), f32, math.log(DT_MIN), math.log(DT_MAX)),
        "b_re": nrm(ks[7], (DEPTH, G, P, C), (2 * C) ** -0.5),
        "b_im": nrm(ks[8], (DEPTH, G, P, C), (2 * C) ** -0.5),
        "c_re": nrm(ks[9], (DEPTH, G, C, P), P ** -0.5),
        "c_im": nrm(ks[10], (DEPTH, G, C, P), P ** -0.5),
        "d_skip": nrm(ks[11], (DEPTH, G, C), 1.0),
        "g_v": gain(ks[12], D_GMLP),
        "w_s": nrm(ks[13], (DEPTH, GMLP_HEADS, CHUNK, CHUNK), CHUNK ** -0.5),
        "b_s": 1.0 + 0.01 * jax.random.normal(ks[14], (DEPTH, GMLP_HEADS, CHUNK), f32),
        "w_out": nrm(ks[15], (DEPTH, D_MIX, D_MODEL), D_MIX ** -0.5),
        "g_mix_post": gain(ks[16], D_MODEL),
        "g_x_pre": gain(ks[17], D_MODEL),
        "g_mem": gain(ks[18], D_MODEL),
        "w_q": nrm(ks[19], (DEPTH, D_MODEL, D_MODEL), D_MODEL ** -0.5),
        "w_k": nrm(ks[20], (DEPTH, D_MODEL, D_MODEL), D_MODEL ** -0.5),
        "w_v": nrm(ks[21], (DEPTH, D_MODEL, D_MODEL), D_MODEL ** -0.5),
        "w_o": nrm(ks[22], (DEPTH, D_MODEL, D_MODEL), D_MODEL ** -0.5),
        "g_x_post": gain(ks[23], D_MODEL),
        "g_ffn_pre": gain(ks[24], D_MODEL),
        "w_up": nrm(ks[25], (DEPTH, D_MODEL, 2 * D_FF), D_MODEL ** -0.5),
        "conv_w": nrm(ks[26], (DEPTH, CONV_WIDTH, 2 * D_FF), CONV_WIDTH ** -0.5),
        "conv_b": nrm(ks[27], (DEPTH, 2 * D_FF), 0.01),
        "w_down": nrm(ks[28], (DEPTH, D_FF, D_MODEL), D_FF ** -0.5),
        "g_ffn_post": gain(ks[29], D_MODEL),
    }


def reference(x, mem, g_mix_pre, w_in, lam_re, lam_im, log_dt, b_re, b_im, c_re, c_im,
              d_skip, g_v, w_s, b_s, w_out, g_mix_post, g_x_pre, g_mem, w_q, w_k, w_v,
              w_o, g_x_post, g_ffn_pre, w_up, conv_w, conv_b, w_down, g_ffn_post):
    splits = [D_SSM, 2 * D_SSM, 2 * D_SSM + D_GMLP]
    for l in range(DEPTH):
        h = rmsnorm(x, g_mix_pre[l])
        s_in, s_gate, g_u, g_vv = jnp.split(h @ w_in[l], splits, axis=-1)
        y_ssm = s5_ssm(s_in, lam_re[l], lam_im[l], log_dt[l], b_re[l], b_im[l],
                       c_re[l], c_im[l], d_skip[l])
        y_ssm = jax.nn.gelu(y_ssm) * jax.nn.sigmoid(s_gate)
        y_gmlp = chunked_spatial_gate(g_u, g_vv, g_v[l], w_s[l], b_s[l])
        mix = jnp.concatenate([y_ssm, y_gmlp], axis=-1) @ w_out[l]
        x = x + rmsnorm(mix, g_mix_post[l])
        h = rmsnorm(x, g_x_pre[l])
        mem_n = rmsnorm(mem, g_mem[l])
        xa = memory_cross_attention(h, mem_n, w_q[l], w_k[l], w_v[l], w_o[l])
        x = x + rmsnorm(xa, g_x_post[l])
        h = rmsnorm(x, g_ffn_pre[l])
        f = conv_ffn(h, w_up[l], conv_w[l], conv_b[l], w_down[l])
        x = x + rmsnorm(f, g_ffn_post[l])
    return x
```

```python
import functools
import math

import jax
import jax.numpy as jnp
from jax import lax
from jax.experimental import pallas as pl
from jax.experimental.pallas import tpu as pltpu

F32 = jnp.float32
BF16 = jnp.bfloat16

D_MODEL = 1024
D_SSM = 512
SSM_GROUP = 16
N_SSM_GROUPS = 32
SSM_STATE = 64
D_GMLP = 512
GMLP_HEADS = 8
GMLP_HEAD_DIM = 64
CHUNK = 128
MEM_LEN = 256
XATTN_HEADS = 4
XATTN_HEAD_DIM = 256
D_FF = 2816
RMS_EPS = 1e-6

LANES = 128
SUBLANES = 8
MXU_DIM = 256
N_SLABS = 2 * N_SSM_GROUPS * SSM_STATE // LANES
VMEM_LIMIT = 56 * 1024 * 1024

MIX_TC = 256
ATT_TC = 512
FFN_TC = 512
FF_TILE = MXU_DIM
N_FF_TILES = D_FF // FF_TILE


def _rmsnorm(xf, g):
    ms = jnp.mean(xf * xf, axis=-1, keepdims=True)
    return xf * lax.rsqrt(ms + RMS_EPS) * g


def _dot(a, b):
    return jnp.dot(a, b, preferred_element_type=F32)


def _const_spec(shape, layer):
    nd = len(shape)
    return pl.BlockSpec((None,) + tuple(shape), lambda *_: (layer,) + (0,) * nd,
                        pipeline_mode=pl.Buffered(1))


def _zoh(lr, li, log_dt):
    dt = jnp.exp(log_dt)
    mag = jnp.exp(lr * dt)
    ab_r = mag * jnp.cos(li * dt)
    ab_i = mag * jnp.sin(li * dt)
    nr, ni = ab_r - 1.0, ab_i
    den = lr * lr + li * li
    fr = (nr * lr + ni * li) / den
    fi = (ni * lr - nr * li) / den
    return ab_r, ab_i, fr, fi


def _prep_kernel(lr_a, li_a, ld_a, lr_b, li_b, ld_b, br_ref, bi_ref, ar_ref, ai_ref, bbr_ref, bbi_ref):
    ab_r, ab_i, _, _ = _zoh(lr_a[...], li_a[...], ld_a[...])
    ar_ref[...] = ab_r
    ai_ref[...] = ab_i
    _, _, fr, fi = _zoh(lr_b[...], li_b[...], ld_b[...])
    br, bi = br_ref[...], bi_ref[...]
    bbr_ref[...] = fr * br - fi * bi
    bbi_ref[...] = fr * bi + fi * br


def _kv_kernel(mem_ref, g_ref, wk_ref, wv_ref, kt_ref, v_ref):
    mn = _rmsnorm(mem_ref[...], g_ref[...]).astype(BF16)
    k = _dot(mn, wk_ref[...])
    kt_ref[...] = k.T.astype(BF16)
    v_ref[...] = _dot(mn, wv_ref[...]).astype(BF16)


def _mixer_kernel(x_ref, gpre_ref, win_ref, wb_ref, ar_ref, ai_ref, wcr_ref, wci_ref, dskip_ref,
                  gv_ref, ones_ref, wsp_ref, bias_ref, wout_ref, gpost_ref,
                  o_ref, st_ref, drv_ref, *, nb, tc):
    m = nb * tc
    nblk = tc // SUBLANES

    @pl.when(pl.program_id(0) == 0)
    def _():
        st_ref[...] = jnp.zeros_like(st_ref)

    x = x_ref[...].reshape(m, D_MODEL)
    h = _rmsnorm(x, gpre_ref[...]).astype(BF16)
    proj = _dot(h, win_ref[...])
    s_in = proj[:, 0:D_SSM]
    s_gate = proj[:, D_SSM:2 * D_SSM]
    g_u = proj[:, 2 * D_SSM:2 * D_SSM + D_GMLP]
    g_vv = proj[:, 2 * D_SSM + D_GMLP:]

    s_in_bf = s_in.astype(BF16)
    n_tiles = N_SLABS * LANES // MXU_DIM
    for j in range(n_tiles):
        ks = LANES * ((j % (n_tiles // 2)) // 2)
        bt = _dot(s_in_bf[:, ks:ks + LANES], wb_ref[ks:ks + LANES, MXU_DIM * j:MXU_DIM * (j + 1)])
        for k in range(2):
            s = 2 * j + k
            drv_ref[:, SUBLANES * s:SUBLANES * (s + 1), :] = (
                bt[:, LANES * k:LANES * (k + 1)].reshape(m // SUBLANES, SUBLANES, LANES))

    ar = (ar_ref[0:8, :], ar_ref[8:16, :])
    ai = (ai_ref[0:8, :], ai_ref[8:16, :])
    half = N_SLABS // 2 * SUBLANES

    def scan_block(tb, carry):
        carry = list(carry)
        for r in range(SUBLANES):
            for b in range(nb):
                row = b * nblk + tb
                for p in range(2):
                    xr, xi = carry[4 * b + p], carry[4 * b + 2 + p]
                    rr = pl.ds(64 * p + r, SUBLANES, stride=SUBLANES)
                    ri = pl.ds(half + 64 * p + r, SUBLANES, stride=SUBLANES)
                    nr = ar[p] * xr - ai[p] * xi + drv_ref[row, rr, :]
                    ni = ar[p] * xi + ai[p] * xr + drv_ref[row, ri, :]
                    drv_ref[row, rr, :] = nr
                    drv_ref[row, ri, :] = ni
                    carry[4 * b + p], carry[4 * b + 2 + p] = nr, ni
        return tuple(carry)

    init = tuple(st_ref[SUBLANES * i:SUBLANES * (i + 1), :] for i in range(4 * nb))
    fin = lax.fori_loop(0, nblk, scan_block, init)
    for i in range(4 * nb):
        st_ref[SUBLANES * i:SUBLANES * (i + 1), :] = fin[i]

    def slab_pair(s):
        parts = [drv_ref[:, SUBLANES * q:SUBLANES * (q + 1), :].reshape(m, LANES) for q in (s, s + 1)]
        return jnp.concatenate(parts, axis=1).astype(BF16)

    ys = []
    for n in range(D_SSM // MXU_DIM):
        accr = acci = None
        for q in range(4):
            sr = 8 * n + 2 * q
            rows = slice(LANES * sr, LANES * sr + MXU_DIM)
            cols = slice(MXU_DIM * n, MXU_DIM * (n + 1))
            dr = _dot(slab_pair(sr), wcr_ref[rows, cols])
            di = _dot(slab_pair(N_SLABS // 2 + sr), wci_ref[rows, cols])
            accr = dr if accr is None else accr + dr
            acci = di if acci is None else acci + di
        ys.append(accr - acci)
    y = jnp.concatenate(ys, axis=1) + dskip_ref[...] * s_in
    y_ssm = jax.nn.gelu(y) * jax.nn.sigmoid(s_gate)

    sq = g_vv * g_vv
    sq_hi = sq.astype(BF16)
    sq_lo = (sq - sq_hi.astype(F32)).astype(BF16)
    ssum = _dot(sq_hi, ones_ref[...]) + _dot(sq_lo, ones_ref[...])
    vn = g_vv * lax.rsqrt(ssum * (1.0 / GMLP_HEAD_DIM) + RMS_EPS) * gv_ref[...]
    lane = lax.broadcasted_iota(jnp.int32, (m, D_GMLP), 1) % LANES
    vn_lo = jnp.where(lane < GMLP_HEAD_DIM, vn, 0.0).astype(BF16)
    vn_hi = jnp.where(lane >= GMLP_HEAD_DIM, vn, 0.0).astype(BF16)
    rowi = lax.broadcasted_iota(jnp.int32, (CHUNK, 2 * CHUNK), 0)
    coli = lax.broadcasted_iota(jnp.int32, (CHUNK, 2 * CHUNK), 1) % CHUNK
    wsm = [jnp.where(rowi >= coli, wsp_ref[k], 0.0).astype(BF16) for k in range(GMLP_HEADS // 2)]
    chunks = []
    for c in range(m // CHUNK):
        rs = slice(CHUNK * c, CHUNK * (c + 1))
        outs = []
        for k in range(GMLP_HEADS // 2):
            cs = slice(LANES * k, LANES * (k + 1))
            rhs = jnp.concatenate([vn_lo[rs, cs], vn_hi[rs, cs]], axis=0)
            outs.append(_dot(wsm[k], rhs))
        chunks.append(jnp.concatenate(outs, axis=1) + bias_ref[...])
    y_gmlp = g_u * jnp.concatenate(chunks, axis=0)

    mix_in = jnp.concatenate([y_ssm, y_gmlp], axis=1).astype(BF16)
    mix = _dot(mix_in, wout_ref[...])
    o_ref[...] = (x + _rmsnorm(mix, gpost_ref[...])).reshape(nb, tc, D_MODEL)


def _xattn_kernel(x_ref, gpre_ref, wq_ref, kt_ref, v_ref, wo_ref, gpost_ref, o_ref):
    x = x_ref[...]
    h = _rmsnorm(x, gpre_ref[...]).astype(BF16)
    q = (_dot(h, wq_ref[...]) * (1.0 / math.sqrt(XATTN_HEAD_DIM))).astype(BF16)
    outs = []
    for hd in range(XATTN_HEADS):
        cs = slice(XATTN_HEAD_DIM * hd, XATTN_HEAD_DIM * (hd + 1))
        s = _dot(q[:, cs], kt_ref[cs, :])
        e = jnp.exp(s - jnp.max(s, axis=-1, keepdims=True))
        p = e / jnp.sum(e, axis=-1, keepdims=True)
        outs.append(_dot(p.astype(BF16), v_ref[:, cs]).astype(BF16))
    xa = _dot(jnp.concatenate(outs, axis=1), wo_ref[...])
    o_ref[...] = x + _rmsnorm(xa, gpost_ref[...])


def _ffn_kernel(x_ref, gpre_ref, wup_ref, cw_ref, cb_ref, wdn_ref, gpost_ref, o_ref,
                carry_ref, acc_ref, *, tc):
    @pl.when(pl.program_id(1) == 0)
    def _():
        carry_ref[...] = jnp.zeros_like(carry_ref)

    x = x_ref[...]
    h = _rmsnorm(x, gpre_ref[...]).astype(BF16)
    row = lax.broadcasted_iota(jnp.int32, (tc, FF_TILE), 0)
    acc_ref[...] = jnp.zeros_like(acc_ref)

    def conv(f):
        a = _dot(h, wup_ref[f])
        prev = carry_ref[f]
        carry_ref[f] = a[tc - SUBLANES:, :]
        p1 = prev[SUBLANES - 1:SUBLANES, :]
        p2 = prev[SUBLANES - 2:SUBLANES - 1, :]
        a1 = jnp.where(row == 0, p1, pltpu.roll(a, 1, 0))
        a2 = jnp.where(row == 0, p2, jnp.where(row == 1, p1, pltpu.roll(a, 2, 0)))
        w = cw_ref[f]
        return w[0:1, :] * a2 + w[1:2, :] * a1 + w[2:3, :] * a + cb_ref[f]

    def tile(f, _):
        val = conv(f)
        gate = conv(N_FF_TILES + f)
        act = (val * jax.nn.gelu(gate)).astype(BF16)
        acc_ref[...] += _dot(act, wdn_ref[f])
        return 0

    lax.fori_loop(0, N_FF_TILES, tile, 0)
    o_ref[...] = x + _rmsnorm(acc_ref[...], gpost_ref[...])


def kernel(x, mem, g_mix_pre, w_in, lam_re, lam_im, log_dt, b_re, b_im, c_re, c_im, d_skip, g_v, w_s, b_s, w_out, g_mix_post, g_x_pre, g_mem, w_q, w_k, w_v, w_o, g_x_post, g_ffn_pre, w_up, conv_w, conv_b, w_down, g_ffn_post):
    bsz, seq, d = x.shape
    depth = w_in.shape[0]
    G, P, C = N_SSM_GROUPS, SSM_STATE, SSM_GROUP
    assert d == D_MODEL and seq % FFN_TC == 0 and seq % MIX_TC == 0 and mem.shape[1] == MEM_LEN

    a_shape = (depth, G * P // LANES, LANES)
    ld_full = jnp.broadcast_to(log_dt[:, :, None], (depth, G, P))
    rep = lambda t: jnp.broadcast_to(t[:, :, None, :], (depth, G, C, P)).reshape(depth, G * C, P)
    gcp = lambda t: jnp.transpose(t, (0, 1, 3, 2)).reshape(depth, G * C, P)
    ar, ai, bbr, bbi = pl.pallas_call(
        _prep_kernel,
        out_shape=[jax.ShapeDtypeStruct(a_shape, F32)] * 2
        + [jax.ShapeDtypeStruct((depth, G * C, P), F32)] * 2,
        name="ssm_prep",
    )(lam_re.reshape(a_shape), lam_im.reshape(a_shape), ld_full.reshape(a_shape),
      rep(lam_re), rep(lam_im), rep(ld_full), gcp(b_re), gcp(b_im))

    eye = jnp.eye(G, dtype=F32)
    bd_in = lambda t: (t.reshape(depth, G, C, 1, P) * eye[None, :, None, :, None]).reshape(depth, G * C, G * P)
    wb = jnp.concatenate([bd_in(bbr), bd_in(bbi)], axis=2).astype(BF16)
    bd_out = lambda t: (jnp.transpose(t, (0, 1, 3, 2))[:, :, :, None, :]
                        * eye[None, :, None, :, None]).reshape(depth, G * P, G * C)
    wcr = bd_out(c_re).astype(BF16)
    wci = bd_out(c_im).astype(BF16)
    dsk = d_skip.reshape(depth, 1, D_SSM)

    vec = lambda t: t.reshape(depth, 1, -1)
    ones_bd = jnp.kron(jnp.eye(GMLP_HEADS, dtype=F32), jnp.ones((GMLP_HEAD_DIM, GMLP_HEAD_DIM), F32)).astype(BF16)
    wsp = (w_s.reshape(depth, GMLP_HEADS // 2, 2, CHUNK, CHUNK).transpose(0, 1, 3, 2, 4)
           .reshape(depth, GMLP_HEADS // 2, CHUNK, 2 * CHUNK))
    bias2d = jnp.repeat(jnp.transpose(b_s, (0, 2, 1)), GMLP_HEAD_DIM, axis=2)

    w_in_b, w_out_b = w_in.astype(BF16), w_out.astype(BF16)
    w_q_b, w_k_b, w_v_b, w_o_b = (t.astype(BF16) for t in (w_q, w_k, w_v, w_o))
    n_up = 2 * N_FF_TILES
    w_up_t = w_up.astype(BF16).reshape(depth, D_MODEL, n_up, FF_TILE).transpose(0, 2, 1, 3)
    cw_t = conv_w.reshape(depth, 3, n_up, FF_TILE).transpose(0, 2, 1, 3)
    cb_t = conv_b.reshape(depth, n_up, 1, FF_TILE)
    w_dn_t = w_down.astype(BF16).reshape(depth, N_FF_TILES, FF_TILE, D_MODEL)

    kt, vv = pl.pallas_call(
        _kv_kernel,
        grid=(depth, bsz),
        in_specs=[
            pl.BlockSpec((None, MEM_LEN, D_MODEL), lambda l, b: (b, 0, 0)),
            pl.BlockSpec((None, 1, D_MODEL), lambda l, b: (l, 0, 0)),
            pl.BlockSpec((None, D_MODEL, D_MODEL), lambda l, b: (l, 0, 0)),
            pl.BlockSpec((None, D_MODEL, D_MODEL), lambda l, b: (l, 0, 0)),
        ],
        out_specs=[
            pl.BlockSpec((None, None, D_MODEL, MEM_LEN), lambda l, b: (l, b, 0, 0)),
            pl.BlockSpec((None, None, MEM_LEN, D_MODEL), lambda l, b: (l, b, 0, 0)),
        ],
        out_shape=[jax.ShapeDtypeStruct((depth, bsz, D_MODEL, MEM_LEN), BF16),
                   jax.ShapeDtypeStruct((depth, bsz, MEM_LEN, D_MODEL), BF16)],
        compiler_params=pltpu.CompilerParams(dimension_semantics=("arbitrary", "arbitrary"),
                                             vmem_limit_bytes=VMEM_LIMIT),
        name="mem_kv",
    )(mem, vec(g_mem), w_k_b, w_v_b)

    x_shape = jax.ShapeDtypeStruct(x.shape, F32)
    for l in range(depth):
        cs = functools.partial(_const_spec, layer=l)
        x = pl.pallas_call(
            functools.partial(_mixer_kernel, nb=bsz, tc=MIX_TC),
            grid=(seq // MIX_TC,),
            in_specs=[
                pl.BlockSpec((bsz, MIX_TC, D_MODEL), lambda t: (0, t, 0)),
                cs((1, D_MODEL)), cs((D_MODEL, 2 * D_SSM + 2 * D_GMLP)), cs((D_SSM, N_SLABS * LANES)),
                cs(a_shape[1:]), cs(a_shape[1:]), cs((G * P, D_SSM)), cs((G * P, D_SSM)), cs((1, D_SSM)),
                cs((1, D_GMLP)),
                pl.BlockSpec((D_GMLP, D_GMLP), lambda t: (0, 0), pipeline_mode=pl.Buffered(1)),
                cs((GMLP_HEADS // 2, CHUNK, 2 * CHUNK)), cs((CHUNK, D_GMLP)),
                cs((D_MODEL, D_MODEL)), cs((1, D_MODEL)),
            ],
            out_specs=pl.BlockSpec((bsz, MIX_TC, D_MODEL), lambda t: (0, t, 0)),
            out_shape=x_shape,
            scratch_shapes=[
                pltpu.VMEM((4 * bsz * SUBLANES, LANES), F32),
                pltpu.VMEM((bsz * MIX_TC // SUBLANES, N_SLABS * SUBLANES, LANES), F32),
            ],
            compiler_params=pltpu.CompilerParams(dimension_semantics=("arbitrary",),
                                                 vmem_limit_bytes=VMEM_LIMIT),
            name="mixer",
        )(x, vec(g_mix_pre), w_in_b, wb, ar, ai, wcr, wci, dsk, vec(g_v), ones_bd, wsp, bias2d,
          w_out_b, vec(g_mix_post))

        x = pl.pallas_call(
            _xattn_kernel,
            grid=(bsz, seq // ATT_TC),
            in_specs=[
                pl.BlockSpec((None, ATT_TC, D_MODEL), lambda b, t: (b, t, 0)),
                cs((1, D_MODEL)), cs((D_MODEL, D_MODEL)),
                pl.BlockSpec((None, None, D_MODEL, MEM_LEN), lambda b, t: (l, b, 0, 0)),
                pl.BlockSpec((None, None, MEM_LEN, D_MODEL), lambda b, t: (l, b, 0, 0)),
                cs((D_MODEL, D_MODEL)), cs((1, D_MODEL)),
            ],
            out_specs=pl.BlockSpec((None, ATT_TC, D_MODEL), lambda b, t: (b, t, 0)),
            out_shape=x_shape,
            compiler_params=pltpu.CompilerParams(dimension_semantics=("arbitrary", "arbitrary"),
                                                 vmem_limit_bytes=VMEM_LIMIT),
            name="xattn",
        )(x, vec(g_x_pre), w_q_b, kt, vv, w_o_b, vec(g_x_post))

        x = pl.pallas_call(
            functools.partial(_ffn_kernel, tc=FFN_TC),
            grid=(bsz, seq // FFN_TC),
            in_specs=[
                pl.BlockSpec((None, FFN_TC, D_MODEL), lambda b, t: (b, t, 0)),
                cs((1, D_MODEL)), cs((n_up, D_MODEL, FF_TILE)), cs((n_up, 3, FF_TILE)),
                cs((n_up, 1, FF_TILE)), cs((N_FF_TILES, FF_TILE, D_MODEL)), cs((1, D_MODEL)),
            ],
            out_specs=pl.BlockSpec((None, FFN_TC, D_MODEL), lambda b, t: (b, t, 0)),
            out_shape=x_shape,
            scratch_shapes=[
                pltpu.VMEM((n_up, SUBLANES, FF_TILE), F32),
                pltpu.VMEM((FFN_TC, D_MODEL), F32),
            ],
            compiler_params=pltpu.CompilerParams(dimension_semantics=("arbitrary", "arbitrary"),
                                                 vmem_limit_bytes=VMEM_LIMIT),
            name="convffn",
        )(x, vec(g_ffn_pre), w_up_t, cw_t, cb_t, w_dn_t, vec(g_ffn_post))
    return x
```

```python
import functools
import math

import jax
import jax.numpy as jnp
from jax import lax
from jax.experimental import pallas as pl
from jax.experimental.pallas import tpu as pltpu

F32 = jnp.float32
BF16 = jnp.bfloat16

D_MODEL = 1024
D_SSM = 512
SSM_GROUP = 16
N_SSM_GROUPS = 32
SSM_STATE = 64
D_GMLP = 512
GMLP_HEADS = 8
GMLP_HEAD_DIM = 64
CHUNK = 128
MEM_LEN = 256
XATTN_HEADS = 4
XATTN_HEAD_DIM = 256
D_FF = 2816
RMS_EPS = 1e-6

LANES = 128
SUBLANES = 8
MXU_DIM = 256
N_SLABS = 2 * N_SSM_GROUPS * SSM_STATE // LANES
VMEM_LIMIT = 56 * 1024 * 1024

MIX_TC = 256
ATT_TC = 512
FFN_TC = 512
FF_TILE = MXU_DIM
FFN_STRIP = 16
N_FF_TILES = D_FF // FF_TILE


def _rmsnorm(xf, g):
    ms = jnp.mean(xf * xf, axis=-1, keepdims=True)
    return xf * lax.rsqrt(ms + RMS_EPS) * g


def _dot(a, b):
    return jnp.dot(a, b, preferred_element_type=F32)


def _const_spec(shape, layer):
    nd = len(shape)
    return pl.BlockSpec((None,) + tuple(shape), lambda *_: (layer,) + (0,) * nd,
                        pipeline_mode=pl.Buffered(1))


def _zoh(lr, li, log_dt):
    dt = jnp.exp(log_dt)
    mag = jnp.exp(lr * dt)
    ab_r = mag * jnp.cos(li * dt)
    ab_i = mag * jnp.sin(li * dt)
    nr, ni = ab_r - 1.0, ab_i
    den = lr * lr + li * li
    fr = (nr * lr + ni * li) / den
    fi = (ni * lr - nr * li) / den
    return ab_r, ab_i, fr, fi


def _prep_kernel(lr_a, li_a, ld_a, lr_b, li_b, ld_b, br_ref, bi_ref, ar_ref, ai_ref, bbr_ref, bbi_ref):
    ab_r, ab_i, _, _ = _zoh(lr_a[...], li_a[...], ld_a[...])
    ar_ref[...] = ab_r
    ai_ref[...] = ab_i
    _, _, fr, fi = _zoh(lr_b[...], li_b[...], ld_b[...])
    br, bi = br_ref[...], bi_ref[...]
    bbr_ref[...] = fr * br - fi * bi
    bbi_ref[...] = fr * bi + fi * br


def _kv_kernel(mem_ref, g_ref, wk_ref, wv_ref, kt_ref, v_ref):
    mn = _rmsnorm(mem_ref[...], g_ref[...]).astype(BF16)
    k = _dot(mn, wk_ref[...])
    kt_ref[...] = k.T.astype(BF16)
    v_ref[...] = _dot(mn, wv_ref[...]).astype(BF16)


def _mixer_kernel(x_ref, gpre_ref, win_ref, wb_ref, ar_ref, ai_ref, wcr_ref, wci_ref, dskip_ref,
                  gv_ref, ones_ref, wsp_ref, bias_ref, wout_ref, gpost_ref,
                  o_ref, st_ref, drv_ref, *, nb, tc):
    m = nb * tc
    nblk = tc // SUBLANES

    @pl.when(pl.program_id(0) == 0)
    def _():
        st_ref[...] = jnp.zeros_like(st_ref)

    x = x_ref[...].reshape(m, D_MODEL)
    h = _rmsnorm(x, gpre_ref[...]).astype(BF16)
    proj = _dot(h, win_ref[...])
    s_in = proj[:, 0:D_SSM]
    s_gate = proj[:, D_SSM:2 * D_SSM]
    g_u = proj[:, 2 * D_SSM:2 * D_SSM + D_GMLP]
    g_vv = proj[:, 2 * D_SSM + D_GMLP:]

    s_in_bf = s_in.astype(BF16)
    n_tiles = N_SLABS * LANES // MXU_DIM
    for j in range(n_tiles):
        ks = LANES * ((j % (n_tiles // 2)) // 2)
        bt = _dot(s_in_bf[:, ks:ks + LANES], wb_ref[ks:ks + LANES, MXU_DIM * j:MXU_DIM * (j + 1)])
        for k in range(2):
            s = 2 * j + k
            drv_ref[:, SUBLANES * s:SUBLANES * (s + 1), :] = (
                bt[:, LANES * k:LANES * (k + 1)].reshape(m // SUBLANES, SUBLANES, LANES))

    ar = (ar_ref[0:8, :], ar_ref[8:16, :])
    ai = (ai_ref[0:8, :], ai_ref[8:16, :])
    half = N_SLABS // 2 * SUBLANES

    def scan_block(tb, carry):
        carry = list(carry)
        for r in range(SUBLANES):
            for b in range(nb):
                row = b * nblk + tb
                for p in range(2):
                    xr, xi = carry[4 * b + p], carry[4 * b + 2 + p]
                    rr = pl.ds(64 * p + r, SUBLANES, stride=SUBLANES)
                    ri = pl.ds(half + 64 * p + r, SUBLANES, stride=SUBLANES)
                    nr = ar[p] * xr - ai[p] * xi + drv_ref[row, rr, :]
                    ni = ar[p] * xi + ai[p] * xr + drv_ref[row, ri, :]
                    drv_ref[row, rr, :] = nr
                    drv_ref[row, ri, :] = ni
                    carry[4 * b + p], carry[4 * b + 2 + p] = nr, ni
        return tuple(carry)

    init = tuple(st_ref[SUBLANES * i:SUBLANES * (i + 1), :] for i in range(4 * nb))
    fin = lax.fori_loop(0, nblk, scan_block, init)
    for i in range(4 * nb):
        st_ref[SUBLANES * i:SUBLANES * (i + 1), :] = fin[i]

    def slab_pair(s):
        parts = [drv_ref[:, SUBLANES * q:SUBLANES * (q + 1), :].reshape(m, LANES) for q in (s, s + 1)]
        return jnp.concatenate(parts, axis=1).astype(BF16)

    ys = []
    for n in range(D_SSM // MXU_DIM):
        accr = acci = None
        for q in range(4):
            sr = 8 * n + 2 * q
            rows = slice(LANES * sr, LANES * sr + MXU_DIM)
            cols = slice(MXU_DIM * n, MXU_DIM * (n + 1))
            dr = _dot(slab_pair(sr), wcr_ref[rows, cols])
            di = _dot(slab_pair(N_SLABS // 2 + sr), wci_ref[rows, cols])
            accr = dr if accr is None else accr + dr
            acci = di if acci is None else acci + di
        ys.append(accr - acci)
    y = jnp.concatenate(ys, axis=1) + dskip_ref[...] * s_in
    y_ssm = jax.nn.gelu(y) * jax.nn.sigmoid(s_gate)

    sq = g_vv * g_vv
    sq_hi = sq.astype(BF16)
    sq_lo = (sq - sq_hi.astype(F32)).astype(BF16)
    ssum = _dot(sq_hi, ones_ref[...]) + _dot(sq_lo, ones_ref[...])
    vn = g_vv * lax.rsqrt(ssum * (1.0 / GMLP_HEAD_DIM) + RMS_EPS) * gv_ref[...]
    lane = lax.broadcasted_iota(jnp.int32, (m, D_GMLP), 1) % LANES
    vn_lo = jnp.where(lane < GMLP_HEAD_DIM, vn, 0.0).astype(BF16)
    vn_hi = jnp.where(lane >= GMLP_HEAD_DIM, vn, 0.0).astype(BF16)
    rowi = lax.broadcasted_iota(jnp.int32, (CHUNK, 2 * CHUNK), 0)
    coli = lax.broadcasted_iota(jnp.int32, (CHUNK, 2 * CHUNK), 1) % CHUNK
    wsm = [jnp.where(rowi >= coli, wsp_ref[k], 0.0).astype(BF16) for k in range(GMLP_HEADS // 2)]
    chunks = []
    for c in range(m // CHUNK):
        rs = slice(CHUNK * c, CHUNK * (c + 1))
        outs = []
        for k in range(GMLP_HEADS // 2):
            cs = slice(LANES * k, LANES * (k + 1))
            rhs = jnp.concatenate([vn_lo[rs, cs], vn_hi[rs, cs]], axis=0)
            outs.append(_dot(wsm[k], rhs))
        chunks.append(jnp.concatenate(outs, axis=1) + bias_ref[...])
    y_gmlp = g_u * jnp.concatenate(chunks, axis=0)

    mix_in = jnp.concatenate([y_ssm, y_gmlp], axis=1).astype(BF16)
    mix = _dot(mix_in, wout_ref[...])
    o_ref[...] = (x + _rmsnorm(mix, gpost_ref[...])).reshape(nb, tc, D_MODEL)


def _xattn_kernel(x_ref, gpre_ref, wq_ref, kt_ref, v_ref, wo_ref, gpost_ref, o_ref):
    x = x_ref[...]
    h = _rmsnorm(x, gpre_ref[...]).astype(BF16)
    q = (_dot(h, wq_ref[...]) * (1.0 / math.sqrt(XATTN_HEAD_DIM))).astype(BF16)
    outs = []
    for hd in range(XATTN_HEADS):
        cs = slice(XATTN_HEAD_DIM * hd, XATTN_HEAD_DIM * (hd + 1))
        s = _dot(q[:, cs], kt_ref[cs, :])
        e = jnp.exp(s - jnp.max(s, axis=-1, keepdims=True))
        p = e / jnp.sum(e, axis=-1, keepdims=True)
        outs.append(_dot(p.astype(BF16), v_ref[:, cs]).astype(BF16))
    xa = _dot(jnp.concatenate(outs, axis=1), wo_ref[...])
    o_ref[...] = x + _rmsnorm(xa, gpost_ref[...])


def _ffn_kernel(x_ref, gpre_ref, wup_ref, cw_ref, cb_ref, wdn_ref, gpost_ref, o_ref,
                carry_ref, h_ref, a0_ref, a1_ref, act0_ref, act1_ref, acc_ref, *, tc):
    @pl.when(pl.program_id(1) == 0)
    def _():
        carry_ref[...] = jnp.zeros_like(carry_ref)

    h_ref[...] = _rmsnorm(x_ref[...], gpre_ref[...]).astype(BF16)
    acc_ref[...] = jnp.zeros_like(acc_ref)
    row = lax.broadcasted_iota(jnp.int32, (SUBLANES, 2 * FF_TILE), 0)
    first1, first2 = row < 1, row < 2

    def up(f, a_ref):
        a_ref[...] = _dot(h_ref[...], wup_ref[f])

    def conv_act(f, a_ref, act_ref):
        bc = lambda v: jnp.broadcast_to(v, (SUBLANES, 2 * FF_TILE))
        w0, w1, w2 = (bc(cw_ref[f, k:k + 1, :]) for k in range(3))
        bias = bc(cb_ref[f])
        prev = carry_ref[f]
        p1, p2 = pltpu.roll(prev, 1, 0), pltpu.roll(prev, 2, 0)
        for r0 in range(0, tc, FFN_STRIP):
            cs = []
            for r in range(r0, r0 + FFN_STRIP, SUBLANES):
                a0 = a_ref[pl.ds(r, SUBLANES), :]
                q1, q2 = pltpu.roll(a0, 1, 0), pltpu.roll(a0, 2, 0)
                a1 = jnp.where(first1, p1, q1)
                a2 = jnp.where(first2, p2, q2)
                cs.append(w0 * a2 + w1 * a1 + w2 * a0 + bias)
                p1, p2 = q1, q2
            c = jnp.concatenate(cs, axis=0)
            act_ref[pl.ds(r0, FFN_STRIP), :] = (
                c[:, :FF_TILE] * jax.nn.gelu(c[:, FF_TILE:])).astype(BF16)
        carry_ref[f] = a_ref[pl.ds(tc - SUBLANES, SUBLANES), :]

    def down(f, act_ref):
        acc_ref[...] += _dot(act_ref[...], wdn_ref[f])

    def step(f, a_cur, act_cur, a_nxt, act_prv):
        up(f + 1, a_nxt)
        conv_act(f, a_cur, act_cur)
        down(f - 1, act_prv)

    up(0, a0_ref)
    up(1, a1_ref)
    conv_act(0, a0_ref, act0_ref)

    def pair(k, c):
        f = 2 * k + 1
        step(f, a1_ref, act1_ref, a0_ref, act0_ref)
        step(f + 1, a0_ref, act0_ref, a1_ref, act1_ref)
        return c

    n_pairs = (N_FF_TILES - 2) // 2
    lax.fori_loop(0, n_pairs, pair, 0)
    step(N_FF_TILES - 2, a1_ref, act1_ref, a0_ref, act0_ref)
    conv_act(N_FF_TILES - 1, a0_ref, act0_ref)
    down(N_FF_TILES - 2, act1_ref)
    down(N_FF_TILES - 1, act0_ref)
    o_ref[...] = x_ref[...] + _rmsnorm(acc_ref[...], gpost_ref[...])


def kernel(x, mem, g_mix_pre, w_in, lam_re, lam_im, log_dt, b_re, b_im, c_re, c_im, d_skip, g_v, w_s, b_s, w_out, g_mix_post, g_x_pre, g_mem, w_q, w_k, w_v, w_o, g_x_post, g_ffn_pre, w_up, conv_w, conv_b, w_down, g_ffn_post):
    bsz, seq, d = x.shape
    depth = w_in.shape[0]
    G, P, C = N_SSM_GROUPS, SSM_STATE, SSM_GROUP
    assert d == D_MODEL and seq % FFN_TC == 0 and seq % MIX_TC == 0 and mem.shape[1] == MEM_LEN
    assert N_FF_TILES % 2 == 1 and N_FF_TILES >= 3

    a_shape = (depth, G * P // LANES, LANES)
    ld_full = jnp.broadcast_to(log_dt[:, :, None], (depth, G, P))
    rep = lambda t: jnp.broadcast_to(t[:, :, None, :], (depth, G, C, P)).reshape(depth, G * C, P)
    gcp = lambda t: jnp.transpose(t, (0, 1, 3, 2)).reshape(depth, G * C, P)
    ar, ai, bbr, bbi = pl.pallas_call(
        _prep_kernel,
        out_shape=[jax.ShapeDtypeStruct(a_shape, F32)] * 2
        + [jax.ShapeDtypeStruct((depth, G * C, P), F32)] * 2,
        name="ssm_prep",
    )(lam_re.reshape(a_shape), lam_im.reshape(a_shape), ld_full.reshape(a_shape),
      rep(lam_re), rep(lam_im), rep(ld_full), gcp(b_re), gcp(b_im))

    def block_diag(rows, n_in, n_out):
        r = lax.broadcasted_iota(jnp.int32, (G * n_in, G * n_out), 0) // n_in
        c = lax.broadcasted_iota(jnp.int32, (G * n_in, G * n_out), 1) // n_out
        return jnp.where(r == c, jnp.tile(rows, (1, 1, G)), 0.0).astype(BF16)

    gpc = lambda t: jnp.transpose(t, (0, 1, 3, 2)).reshape(depth, G * P, C)
    wb = jnp.concatenate([block_diag(bbr, C, P), block_diag(bbi, C, P)], axis=2)
    wcr = block_diag(gpc(c_re), P, C)
    wci = block_diag(gpc(c_im), P, C)
    dsk = d_skip.reshape(depth, 1, D_SSM)

    vec = lambda t: t.reshape(depth, 1, -1)
    ones_bd = jnp.kron(jnp.eye(GMLP_HEADS, dtype=F32), jnp.ones((GMLP_HEAD_DIM, GMLP_HEAD_DIM), F32)).astype(BF16)
    wsp = (w_s.reshape(depth, GMLP_HEADS // 2, 2, CHUNK, CHUNK).transpose(0, 1, 3, 2, 4)
           .reshape(depth, GMLP_HEADS // 2, CHUNK, 2 * CHUNK))
    bias2d = jnp.repeat(jnp.transpose(b_s, (0, 2, 1)), GMLP_HEAD_DIM, axis=2)

    w_in_b, w_out_b = w_in.astype(BF16), w_out.astype(BF16)
    w_q_b, w_k_b, w_v_b, w_o_b = (t.astype(BF16) for t in (w_q, w_k, w_v, w_o))
    ff_tiles = lambda t: (t.reshape(t.shape[:-1] + (2, N_FF_TILES, FF_TILE))
                          .swapaxes(-3, -2).reshape(t.shape[:-1] + (N_FF_TILES, 2 * FF_TILE)))
    w_up_t = jnp.moveaxis(ff_tiles(w_up), 2, 1).astype(BF16)
    cw_t = jnp.moveaxis(ff_tiles(conv_w), 2, 1)
    cb_t = ff_tiles(conv_b)[:, :, None, :]
    w_dn_t = w_down.astype(BF16).reshape(depth, N_FF_TILES, FF_TILE, D_MODEL)

    kt, vv = pl.pallas_call(
        _kv_kernel,
        grid=(depth, bsz),
        in_specs=[
            pl.BlockSpec((None, MEM_LEN, D_MODEL), lambda l, b: (b, 0, 0)),
            pl.BlockSpec((None, 1, D_MODEL), lambda l, b: (l, 0, 0)),
            pl.BlockSpec((None, D_MODEL, D_MODEL), lambda l, b: (l, 0, 0)),
            pl.BlockSpec((None, D_MODEL, D_MODEL), lambda l, b: (l, 0, 0)),
        ],
        out_specs=[
            pl.BlockSpec((None, None, D_MODEL, MEM_LEN), lambda l, b: (l, b, 0, 0)),
            pl.BlockSpec((None, None, MEM_LEN, D_MODEL), lambda l, b: (l, b, 0, 0)),
        ],
        out_shape=[jax.ShapeDtypeStruct((depth, bsz, D_MODEL, MEM_LEN), BF16),
                   jax.ShapeDtypeStruct((depth, bsz, MEM_LEN, D_MODEL), BF16)],
        compiler_params=pltpu.CompilerParams(dimension_semantics=("arbitrary", "arbitrary"),
                                             vmem_limit_bytes=VMEM_LIMIT),
        name="mem_kv",
    )(mem, vec(g_mem), w_k_b, w_v_b)

    x_shape = jax.ShapeDtypeStruct(x.shape, F32)
    for l in range(depth):
        cs = functools.partial(_const_spec, layer=l)
        x = pl.pallas_call(
            functools.partial(_mixer_kernel, nb=bsz, tc=MIX_TC),
            grid=(seq // MIX_TC,),
            in_specs=[
                pl.BlockSpec((bsz, MIX_TC, D_MODEL), lambda t: (0, t, 0)),
                cs((1, D_MODEL)), cs((D_MODEL, 2 * D_SSM + 2 * D_GMLP)), cs((D_SSM, N_SLABS * LANES)),
                cs(a_shape[1:]), cs(a_shape[1:]), cs((G * P, D_SSM)), cs((G * P, D_SSM)), cs((1, D_SSM)),
                cs((1, D_GMLP)),
                pl.BlockSpec((D_GMLP, D_GMLP), lambda t: (0, 0), pipeline_mode=pl.Buffered(1)),
                cs((GMLP_HEADS // 2, CHUNK, 2 * CHUNK)), cs((CHUNK, D_GMLP)),
                cs((D_MODEL, D_MODEL)), cs((1, D_MODEL)),
            ],
            out_specs=pl.BlockSpec((bsz, MIX_TC, D_MODEL), lambda t: (0, t, 0)),
            out_shape=x_shape,
            scratch_shapes=[
                pltpu.VMEM((4 * bsz * SUBLANES, LANES), F32),
                pltpu.VMEM((bsz * MIX_TC // SUBLANES, N_SLABS * SUBLANES, LANES), F32),
            ],
            compiler_params=pltpu.CompilerParams(dimension_semantics=("arbitrary",),
                                                 vmem_limit_bytes=VMEM_LIMIT),
            name="mixer",
        )(x, vec(g_mix_pre), w_in_b, wb, ar, ai, wcr, wci, dsk, vec(g_v), ones_bd, wsp, bias2d,
          w_out_b, vec(g_mix_post))

        x = pl.pallas_call(
            _xattn_kernel,
            grid=(bsz, seq // ATT_TC),
            in_specs=[
                pl.BlockSpec((None, ATT_TC, D_MODEL), lambda b, t: (b, t, 0)),
                cs((1, D_MODEL)), cs((D_MODEL, D_MODEL)),
                pl.BlockSpec((None, None, D_MODEL, MEM_LEN), lambda b, t: (l, b, 0, 0)),
                pl.BlockSpec((None, None, MEM_LEN, D_MODEL), lambda b, t: (l, b, 0, 0)),
                cs((D_MODEL, D_MODEL)), cs((1, D_MODEL)),
            ],
            out_specs=pl.BlockSpec((None, ATT_TC, D_MODEL), lambda b, t: (b, t, 0)),
            out_shape=x_shape,
            compiler_params=pltpu.CompilerParams(dimension_semantics=("arbitrary", "arbitrary"),
                                                 vmem_limit_bytes=VMEM_LIMIT),
            name="xattn",
        )(x, vec(g_x_pre), w_q_b, kt, vv, w_o_b, vec(g_x_post))

        x = pl.pallas_call(
            functools.partial(_ffn_kernel, tc=FFN_TC),
            grid=(bsz, seq // FFN_TC),
            in_specs=[
                pl.BlockSpec((None, FFN_TC, D_MODEL), lambda b, t: (b, t, 0)),
                cs((1, D_MODEL)), cs((N_FF_TILES, D_MODEL, 2 * FF_TILE)), cs((N_FF_TILES, 3, 2 * FF_TILE)),
                cs((N_FF_TILES, 1, 2 * FF_TILE)), cs((N_FF_TILES, FF_TILE, D_MODEL)), cs((1, D_MODEL)),
            ],
            out_specs=pl.BlockSpec((None, FFN_TC, D_MODEL), lambda b, t: (b, t, 0)),
            out_shape=x_shape,
            scratch_shapes=[
                pltpu.VMEM((N_FF_TILES, SUBLANES, 2 * FF_TILE), F32),
                pltpu.VMEM((FFN_TC, D_MODEL), BF16),
                pltpu.VMEM((FFN_TC, 2 * FF_TILE), F32),
                pltpu.VMEM((FFN_TC, 2 * FF_TILE), F32),
                pltpu.VMEM((FFN_TC, FF_TILE), BF16),
                pltpu.VMEM((FFN_TC, FF_TILE), BF16),
                pltpu.VMEM((FFN_TC, D_MODEL), F32),
            ],
            compiler_params=pltpu.CompilerParams(dimension_semantics=("arbitrary", "arbitrary"),
                                                 vmem_limit_bytes=VMEM_LIMIT),
            name="convffn",
        )(x, vec(g_ffn_pre), w_up_t, cw_t, cb_t, w_dn_t, vec(g_ffn_post))
    return x
```

```python
import functools
import math

import jax
import jax.numpy as jnp
from jax import lax
from jax.experimental import pallas as pl
from jax.experimental.pallas import tpu as pltpu

F32 = jnp.float32
BF16 = jnp.bfloat16

D_MODEL = 1024
D_SSM = 512
SSM_GROUP = 16
N_SSM_GROUPS = 32
SSM_STATE = 64
D_GMLP = 512
GMLP_HEADS = 8
GMLP_HEAD_DIM = 64
CHUNK = 128
MEM_LEN = 256
XATTN_HEADS = 4
XATTN_HEAD_DIM = 256
D_FF = 2816
RMS_EPS = 1e-6

LANES = 128
SUBLANES = 8
MXU_DIM = 256
N_SLABS = 2 * N_SSM_GROUPS * SSM_STATE // LANES
VMEM_LIMIT = 56 * 1024 * 1024

MIX_TC = 256
ATT_TC = 512
FFN_TC = 512
FF_TILE = MXU_DIM
FFN_STRIP = 16
N_FF_TILES = D_FF // FF_TILE


def _rmsnorm(xf, g):
    ms = jnp.mean(xf * xf, axis=-1, keepdims=True)
    return xf * lax.rsqrt(ms + RMS_EPS) * g


def _dot(a, b):
    return jnp.dot(a, b, preferred_element_type=F32)


def _const_spec(shape, layer):
    nd = len(shape)
    return pl.BlockSpec((None,) + tuple(shape), lambda *_: (layer,) + (0,) * nd,
                        pipeline_mode=pl.Buffered(1))


def _zoh(lr, li, log_dt):
    dt = jnp.exp(log_dt)
    mag = jnp.exp(lr * dt)
    ab_r = mag * jnp.cos(li * dt)
    ab_i = mag * jnp.sin(li * dt)
    nr, ni = ab_r - 1.0, ab_i
    den = lr * lr + li * li
    fr = (nr * lr + ni * li) / den
    fi = (ni * lr - nr * li) / den
    return ab_r, ab_i, fr, fi


def _prep_kernel(lr_a, li_a, ld_a, lr_b, li_b, ld_b, br_ref, bi_ref, ar_ref, ai_ref, bbr_ref, bbi_ref):
    ab_r, ab_i, _, _ = _zoh(lr_a[...], li_a[...], ld_a[...])
    ar_ref[...] = ab_r
    ai_ref[...] = ab_i
    _, _, fr, fi = _zoh(lr_b[...], li_b[...], ld_b[...])
    br, bi = br_ref[...], bi_ref[...]
    bbr_ref[...] = fr * br - fi * bi
    bbi_ref[...] = fr * bi + fi * br


def _kv_kernel(mem_ref, g_ref, wk_ref, wv_ref, kt_ref, v_ref):
    mn = _rmsnorm(mem_ref[...], g_ref[...]).astype(BF16)
    k = _dot(mn, wk_ref[...])
    kt_ref[...] = k.T.astype(BF16)
    v_ref[...] = _dot(mn, wv_ref[...]).astype(BF16)


def _mixer_kernel(x_ref, gpre_ref, win_ref, wb_ref, ar_ref, ai_ref, wcr_ref, wci_ref, dskip_ref,
                  gv_ref, ones_ref, wsp_ref, bias_ref, wout_ref, gpost_ref,
                  o_ref, st_ref, drv_ref, *, nb, tc):
    m = nb * tc
    nblk = tc // SUBLANES

    @pl.when(pl.program_id(0) == 0)
    def _():
        st_ref[...] = jnp.zeros_like(st_ref)

    x = x_ref[...].reshape(m, D_MODEL)
    h = _rmsnorm(x, gpre_ref[...]).astype(BF16)
    proj = _dot(h, win_ref[...])
    s_in = proj[:, 0:D_SSM]
    s_gate = proj[:, D_SSM:2 * D_SSM]
    g_u = proj[:, 2 * D_SSM:2 * D_SSM + D_GMLP]
    g_vv = proj[:, 2 * D_SSM + D_GMLP:]

    s_in_bf = s_in.astype(BF16)
    n_tiles = N_SLABS * LANES // MXU_DIM
    for j in range(n_tiles):
        ks = LANES * ((j % (n_tiles // 2)) // 2)
        bt = _dot(s_in_bf[:, ks:ks + LANES], wb_ref[ks:ks + LANES, MXU_DIM * j:MXU_DIM * (j + 1)])
        for k in range(2):
            s = 2 * j + k
            drv_ref[:, SUBLANES * s:SUBLANES * (s + 1), :] = (
                bt[:, LANES * k:LANES * (k + 1)].reshape(m // SUBLANES, SUBLANES, LANES))

    ar = (ar_ref[0:8, :], ar_ref[8:16, :])
    ai = (ai_ref[0:8, :], ai_ref[8:16, :])
    half = N_SLABS // 2 * SUBLANES

    def scan_block(tb, carry):
        carry = list(carry)
        for r in range(SUBLANES):
            for b in range(nb):
                row = b * nblk + tb
                for p in range(2):
                    xr, xi = carry[4 * b + p], carry[4 * b + 2 + p]
                    rr = pl.ds(64 * p + r, SUBLANES, stride=SUBLANES)
                    ri = pl.ds(half + 64 * p + r, SUBLANES, stride=SUBLANES)
                    nr = ar[p] * xr - ai[p] * xi + drv_ref[row, rr, :]
                    ni = ar[p] * xi + ai[p] * xr + drv_ref[row, ri, :]
                    drv_ref[row, rr, :] = nr
                    drv_ref[row, ri, :] = ni
                    carry[4 * b + p], carry[4 * b + 2 + p] = nr, ni
        return tuple(carry)

    init = tuple(st_ref[SUBLANES * i:SUBLANES * (i + 1), :] for i in range(4 * nb))
    fin = lax.fori_loop(0, nblk, scan_block, init)
    for i in range(4 * nb):
        st_ref[SUBLANES * i:SUBLANES * (i + 1), :] = fin[i]

    def slab_pair(s):
        parts = [drv_ref[:, SUBLANES * q:SUBLANES * (q + 1), :].reshape(m, LANES) for q in (s, s + 1)]
        return jnp.concatenate(parts, axis=1).astype(BF16)

    ys = []
    for n in range(D_SSM // MXU_DIM):
        accr = acci = None
        for q in range(4):
            sr = 8 * n + 2 * q
            rows = slice(LANES * sr, LANES * sr + MXU_DIM)
            cols = slice(MXU_DIM * n, MXU_DIM * (n + 1))
            dr = _dot(slab_pair(sr), wcr_ref[rows, cols])
            di = _dot(slab_pair(N_SLABS // 2 + sr), wci_ref[rows, cols])
            accr = dr if accr is None else accr + dr
            acci = di if acci is None else acci + di
        ys.append(accr - acci)
    y = jnp.concatenate(ys, axis=1) + dskip_ref[...] * s_in
    y_ssm = jax.nn.gelu(y) * jax.nn.sigmoid(s_gate)

    sq = g_vv * g_vv
    sq_hi = sq.astype(BF16)
    sq_lo = (sq - sq_hi.astype(F32)).astype(BF16)
    ssum = _dot(sq_hi, ones_ref[...]) + _dot(sq_lo, ones_ref[...])
    vn = g_vv * lax.rsqrt(ssum * (1.0 / GMLP_HEAD_DIM) + RMS_EPS) * gv_ref[...]
    lane = lax.broadcasted_iota(jnp.int32, (m, D_GMLP), 1) % LANES
    vn_lo = jnp.where(lane < GMLP_HEAD_DIM, vn, 0.0).astype(BF16)
    vn_hi = jnp.where(lane >= GMLP_HEAD_DIM, vn, 0.0).astype(BF16)
    rowi = lax.broadcasted_iota(jnp.int32, (CHUNK, 2 * CHUNK), 0)
    coli = lax.broadcasted_iota(jnp.int32, (CHUNK, 2 * CHUNK), 1) % CHUNK
    wsm = [jnp.where(rowi >= coli, wsp_ref[k], 0.0).astype(BF16) for k in range(GMLP_HEADS // 2)]
    chunks = []
    for c in range(m // CHUNK):
        rs = slice(CHUNK * c, CHUNK * (c + 1))
        outs = []
        for k in range(GMLP_HEADS // 2):
            cs = slice(LANES * k, LANES * (k + 1))
            rhs = jnp.concatenate([vn_lo[rs, cs], vn_hi[rs, cs]], axis=0)
            outs.append(_dot(wsm[k], rhs))
        chunks.append(jnp.concatenate(outs, axis=1) + bias_ref[...])
    y_gmlp = g_u * jnp.concatenate(chunks, axis=0)

    mix_in = jnp.concatenate([y_ssm, y_gmlp], axis=1).astype(BF16)
    mix = _dot(mix_in, wout_ref[...])
    o_ref[...] = (x + _rmsnorm(mix, gpost_ref[...])).reshape(nb, tc, D_MODEL)


def _xattn_kernel(x_ref, gpre_ref, wq_ref, kt_ref, v_ref, wo_ref, gpost_ref, o_ref):
    x = x_ref[...]
    h = _rmsnorm(x, gpre_ref[...]).astype(BF16)
    q = (_dot(h, wq_ref[...]) * (1.0 / math.sqrt(XATTN_HEAD_DIM))).astype(BF16)
    outs = []
    for hd in range(XATTN_HEADS):
        cs = slice(XATTN_HEAD_DIM * hd, XATTN_HEAD_DIM * (hd + 1))
        s = _dot(q[:, cs], kt_ref[cs, :])
        e = jnp.exp(s - jnp.max(s, axis=-1, keepdims=True))
        p = e / jnp.sum(e, axis=-1, keepdims=True)
        outs.append(_dot(p.astype(BF16), v_ref[:, cs]).astype(BF16))
    xa = _dot(jnp.concatenate(outs, axis=1), wo_ref[...])
    o_ref[...] = x + _rmsnorm(xa, gpost_ref[...])


def _ffn_kernel(x_ref, gpre_ref, wup_ref, cw_ref, cb_ref, wdn_ref, gpost_ref, o_ref,
                carry_ref, h_ref, a0_ref, a1_ref, act0_ref, act1_ref, acc_ref, *, tc):
    @pl.when(pl.program_id(1) == 0)
    def _():
        carry_ref[...] = jnp.zeros_like(carry_ref)

    h_ref[...] = _rmsnorm(x_ref[...], gpre_ref[...]).astype(BF16)
    acc_ref[...] = jnp.zeros_like(acc_ref)
    pad = SUBLANES
    half = FF_TILE // LANES

    def up(f, a_ref):
        for part in range(2):
            col = pl.multiple_of(part * D_FF + f * FF_TILE, FF_TILE)
            res = _dot(h_ref[...], wup_ref[:, pl.ds(col, FF_TILE)])
            for kk in range(half):
                k = part * half + kk
                a_ref[k, 0:pad, :] = carry_ref[f, k]
                a_ref[k, pad:pad + tc, :] = res[:, kk * LANES:(kk + 1) * LANES]
                carry_ref[f, k] = a_ref[k, tc:tc + pad, :]

    def conv_act(f, a_ref, act_ref):
        for kk in range(half):
            slabs = (kk, half + kk)
            taps = [[jnp.broadcast_to(cw_ref[f, j:j + 1, k * LANES:(k + 1) * LANES], (SUBLANES, LANES))
                     for j in range(3)] for k in slabs]
            bias = [jnp.broadcast_to(cb_ref[f, :, k * LANES:(k + 1) * LANES], (SUBLANES, LANES))
                    for k in slabs]
            for r0 in range(0, tc, FFN_STRIP):
                cs = [[], []]
                for r in range(r0, r0 + FFN_STRIP, SUBLANES):
                    for i, k in enumerate(slabs):
                        a2 = a_ref[k, pl.ds(pad + r - 2, SUBLANES), :]
                        a1 = a_ref[k, pl.ds(pad + r - 1, SUBLANES), :]
                        a0 = a_ref[k, pl.ds(pad + r, SUBLANES), :]
                        cs[i].append(taps[i][0] * a2 + taps[i][1] * a1 + taps[i][2] * a0 + bias[i])
                val = jnp.concatenate(cs[0], axis=0)
                gate = jnp.concatenate(cs[1], axis=0)
                act_ref[pl.ds(r0, FFN_STRIP), kk * LANES:(kk + 1) * LANES] = (
                    val * jax.nn.gelu(gate)).astype(BF16)

    def down(f, act_ref):
        acc_ref[...] += _dot(act_ref[...], wdn_ref[f])

    def step(f, a_cur, act_cur, a_nxt, act_prv):
        up(f + 1, a_nxt)
        down(f - 1, act_prv)
        conv_act(f, a_cur, act_cur)

    up(0, a0_ref)
    up(1, a1_ref)
    conv_act(0, a0_ref, act0_ref)

    def pair(k, c):
        f = 2 * k + 1
        step(f, a1_ref, act1_ref, a0_ref, act0_ref)
        step(f + 1, a0_ref, act0_ref, a1_ref, act1_ref)
        return c

    n_pairs = (N_FF_TILES - 2) // 2
    lax.fori_loop(0, n_pairs, pair, 0)
    step(N_FF_TILES - 2, a1_ref, act1_ref, a0_ref, act0_ref)
    conv_act(N_FF_TILES - 1, a0_ref, act0_ref)
    down(N_FF_TILES - 2, act1_ref)
    down(N_FF_TILES - 1, act0_ref)
    o_ref[...] = x_ref[...] + _rmsnorm(acc_ref[...], gpost_ref[...])


def kernel(x, mem, g_mix_pre, w_in, lam_re, lam_im, log_dt, b_re, b_im, c_re, c_im, d_skip, g_v, w_s, b_s, w_out, g_mix_post, g_x_pre, g_mem, w_q, w_k, w_v, w_o, g_x_post, g_ffn_pre, w_up, conv_w, conv_b, w_down, g_ffn_post):
    bsz, seq, d = x.shape
    depth = w_in.shape[0]
    G, P, C = N_SSM_GROUPS, SSM_STATE, SSM_GROUP
    assert d == D_MODEL and seq % FFN_TC == 0 and seq % MIX_TC == 0 and mem.shape[1] == MEM_LEN
    assert N_FF_TILES % 2 == 1 and N_FF_TILES >= 3

    a_shape = (depth, G * P // LANES, LANES)
    ld_full = jnp.broadcast_to(log_dt[:, :, None], (depth, G, P))
    rep = lambda t: jnp.broadcast_to(t[:, :, None, :], (depth, G, C, P)).reshape(depth, G * C, P)
    gcp = lambda t: jnp.transpose(t, (0, 1, 3, 2)).reshape(depth, G * C, P)
    ar, ai, bbr, bbi = pl.pallas_call(
        _prep_kernel,
        out_shape=[jax.ShapeDtypeStruct(a_shape, F32)] * 2
        + [jax.ShapeDtypeStruct((depth, G * C, P), F32)] * 2,
        name="ssm_prep",
    )(lam_re.reshape(a_shape), lam_im.reshape(a_shape), ld_full.reshape(a_shape),
      rep(lam_re), rep(lam_im), rep(ld_full), gcp(b_re), gcp(b_im))

    def block_diag(rows, n_in, n_out):
        r = lax.broadcasted_iota(jnp.int32, (G * n_in, G * n_out), 0) // n_in
        c = lax.broadcasted_iota(jnp.int32, (G * n_in, G * n_out), 1) // n_out
        return jnp.where(r == c, jnp.tile(rows, (1, 1, G)), 0.0).astype(BF16)

    gpc = lambda t: jnp.transpose(t, (0, 1, 3, 2)).reshape(depth, G * P, C)
    wb = jnp.concatenate([block_diag(bbr, C, P), block_diag(bbi, C, P)], axis=2)
    wcr = block_diag(gpc(c_re), P, C)
    wci = block_diag(gpc(c_im), P, C)
    dsk = d_skip.reshape(depth, 1, D_SSM)

    vec = lambda t: t.reshape(depth, 1, -1)
    ones_bd = jnp.kron(jnp.eye(GMLP_HEADS, dtype=F32), jnp.ones((GMLP_HEAD_DIM, GMLP_HEAD_DIM), F32)).astype(BF16)
    wsp = (w_s.reshape(depth, GMLP_HEADS // 2, 2, CHUNK, CHUNK).transpose(0, 1, 3, 2, 4)
           .reshape(depth, GMLP_HEADS // 2, CHUNK, 2 * CHUNK))
    bias2d = jnp.repeat(jnp.transpose(b_s, (0, 2, 1)), GMLP_HEAD_DIM, axis=2)

    w_in_b, w_out_b = w_in.astype(BF16), w_out.astype(BF16)
    w_q_b, w_k_b, w_v_b, w_o_b = (t.astype(BF16) for t in (w_q, w_k, w_v, w_o))
    ff_tiles = lambda t: (t.reshape(t.shape[:-1] + (2, N_FF_TILES, FF_TILE))
                          .swapaxes(-3, -2).reshape(t.shape[:-1] + (N_FF_TILES, 2 * FF_TILE)))
    w_up_b = w_up.astype(BF16)
    cw_t = jnp.moveaxis(ff_tiles(conv_w), 2, 1)
    cb_t = ff_tiles(conv_b)[:, :, None, :]
    w_dn_t = w_down.astype(BF16).reshape(depth, N_FF_TILES, FF_TILE, D_MODEL)

    kt, vv = pl.pallas_call(
        _kv_kernel,
        grid=(depth, bsz),
        in_specs=[
            pl.BlockSpec((None, MEM_LEN, D_MODEL), lambda l, b: (b, 0, 0)),
            pl.BlockSpec((None, 1, D_MODEL), lambda l, b: (l, 0, 0)),
            pl.BlockSpec((None, D_MODEL, D_MODEL), lambda l, b: (l, 0, 0)),
            pl.BlockSpec((None, D_MODEL, D_MODEL), lambda l, b: (l, 0, 0)),
        ],
        out_specs=[
            pl.BlockSpec((None, None, D_MODEL, MEM_LEN), lambda l, b: (l, b, 0, 0)),
            pl.BlockSpec((None, None, MEM_LEN, D_MODEL), lambda l, b: (l, b, 0, 0)),
        ],
        out_shape=[jax.ShapeDtypeStruct((depth, bsz, D_MODEL, MEM_LEN), BF16),
                   jax.ShapeDtypeStruct((depth, bsz, MEM_LEN, D_MODEL), BF16)],
        compiler_params=pltpu.CompilerParams(dimension_semantics=("arbitrary", "arbitrary"),
                                             vmem_limit_bytes=VMEM_LIMIT),
        name="mem_kv",
    )(mem, vec(g_mem), w_k_b, w_v_b)

    x_shape = jax.ShapeDtypeStruct(x.shape, F32)
    for l in range(depth):
        cs = functools.partial(_const_spec, layer=l)
        x = pl.pallas_call(
            functools.partial(_mixer_kernel, nb=bsz, tc=MIX_TC),
            grid=(seq // MIX_TC,),
            in_specs=[
                pl.BlockSpec((bsz, MIX_TC, D_MODEL), lambda t: (0, t, 0)),
                cs((1, D_MODEL)), cs((D_MODEL, 2 * D_SSM + 2 * D_GMLP)), cs((D_SSM, N_SLABS * LANES)),
                cs(a_shape[1:]), cs(a_shape[1:]), cs((G * P, D_SSM)), cs((G * P, D_SSM)), cs((1, D_SSM)),
                cs((1, D_GMLP)),
                pl.BlockSpec((D_GMLP, D_GMLP), lambda t: (0, 0), pipeline_mode=pl.Buffered(1)),
                cs((GMLP_HEADS // 2, CHUNK, 2 * CHUNK)), cs((CHUNK, D_GMLP)),
                cs((D_MODEL, D_MODEL)), cs((1, D_MODEL)),
            ],
            out_specs=pl.BlockSpec((bsz, MIX_TC, D_MODEL), lambda t: (0, t, 0)),
            out_shape=x_shape,
            scratch_shapes=[
                pltpu.VMEM((4 * bsz * SUBLANES, LANES), F32),
                pltpu.VMEM((bsz * MIX_TC // SUBLANES, N_SLABS * SUBLANES, LANES), F32),
            ],
            compiler_params=pltpu.CompilerParams(dimension_semantics=("arbitrary",),
                                                 vmem_limit_bytes=VMEM_LIMIT),
            name="mixer",
        )(x, vec(g_mix_pre), w_in_b, wb, ar, ai, wcr, wci, dsk, vec(g_v), ones_bd, wsp, bias2d,
          w_out_b, vec(g_mix_post))

        x = pl.pallas_call(
            _xattn_kernel,
            grid=(bsz, seq // ATT_TC),
            in_specs=[
                pl.BlockSpec((None, ATT_TC, D_MODEL), lambda b, t: (b, t, 0)),
                cs((1, D_MODEL)), cs((D_MODEL, D_MODEL)),
                pl.BlockSpec((None, None, D_MODEL, MEM_LEN), lambda b, t: (l, b, 0, 0)),
                pl.BlockSpec((None, None, MEM_LEN, D_MODEL), lambda b, t: (l, b, 0, 0)),
                cs((D_MODEL, D_MODEL)), cs((1, D_MODEL)),
            ],
            out_specs=pl.BlockSpec((None, ATT_TC, D_MODEL), lambda b, t: (b, t, 0)),
            out_shape=x_shape,
            compiler_params=pltpu.CompilerParams(dimension_semantics=("arbitrary", "arbitrary"),
                                                 vmem_limit_bytes=VMEM_LIMIT),
            name="xattn",
        )(x, vec(g_x_pre), w_q_b, kt, vv, w_o_b, vec(g_x_post))

        x = pl.pallas_call(
            functools.partial(_ffn_kernel, tc=FFN_TC),
            grid=(bsz, seq // FFN_TC),
            in_specs=[
                pl.BlockSpec((None, FFN_TC, D_MODEL), lambda b, t: (b, t, 0)),
                cs((1, D_MODEL)), cs((D_MODEL, 2 * D_FF)), cs((N_FF_TILES, 3, 2 * FF_TILE)),
                cs((N_FF_TILES, 1, 2 * FF_TILE)), cs((N_FF_TILES, FF_TILE, D_MODEL)), cs((1, D_MODEL)),
            ],
            out_specs=pl.BlockSpec((None, FFN_TC, D_MODEL), lambda b, t: (b, t, 0)),
            out_shape=x_shape,
            scratch_shapes=[
                pltpu.VMEM((N_FF_TILES, 2 * FF_TILE // LANES, SUBLANES, LANES), F32),
                pltpu.VMEM((FFN_TC, D_MODEL), BF16),
                pltpu.VMEM((2 * FF_TILE // LANES, SUBLANES + FFN_TC, LANES), F32),
                pltpu.VMEM((2 * FF_TILE // LANES, SUBLANES + FFN_TC, LANES), F32),
                pltpu.VMEM((FFN_TC, FF_TILE), BF16),
                pltpu.VMEM((FFN_TC, FF_TILE), BF16),
                pltpu.VMEM((FFN_TC, D_MODEL), F32),
            ],
            compiler_params=pltpu.CompilerParams(dimension_semantics=("arbitrary", "arbitrary"),
                                                 vmem_limit_bytes=VMEM_LIMIT),
            name="convffn",
        )(x, vec(g_ffn_pre), w_up_b, cw_t, cb_t, w_dn_t, vec(g_ffn_post))
    return x
```

```python
import functools
import math

import jax
import jax.numpy as jnp
from jax import lax
from jax.experimental import pallas as pl
from jax.experimental.pallas import tpu as pltpu

F32 = jnp.float32
BF16 = jnp.bfloat16

D_MODEL = 1024
D_SSM = 512
SSM_GROUP = 16
N_SSM_GROUPS = 32
SSM_STATE = 64
D_GMLP = 512
GMLP_HEADS = 8
GMLP_HEAD_DIM = 64
CHUNK = 128
MEM_LEN = 256
XATTN_HEADS = 4
XATTN_HEAD_DIM = 256
D_FF = 2816
RMS_EPS = 1e-6
GELU_K0 = math.sqrt(2.0 / math.pi)
GELU_K1 = GELU_K0 * 0.044715

LANES = 128
SUBLANES = 8
MXU_DIM = 256
N_SLABS = 2 * N_SSM_GROUPS * SSM_STATE // LANES
VMEM_LIMIT = 56 * 1024 * 1024

MIX_TC = 256
ATT_TC = 1024
ATT_SPLIT = 2
ATT_STRIP = 16
FFN_TC = 512
FF_TILE = MXU_DIM
FFN_STRIP = 16
N_FF_TILES = D_FF // FF_TILE


def _rmsnorm(xf, g):
    ms = jnp.mean(xf * xf, axis=-1, keepdims=True)
    return xf * lax.rsqrt(ms + RMS_EPS) * g


def _dot(a, b):
    return jnp.dot(a, b, preferred_element_type=F32)


def _const_spec(shape, layer):
    nd = len(shape)
    return pl.BlockSpec((None,) + tuple(shape), lambda *_: (layer,) + (0,) * nd,
                        pipeline_mode=pl.Buffered(1))


def _zoh(lr, li, log_dt):
    dt = jnp.exp(log_dt)
    mag = jnp.exp(lr * dt)
    ab_r = mag * jnp.cos(li * dt)
    ab_i = mag * jnp.sin(li * dt)
    nr, ni = ab_r - 1.0, ab_i
    den = lr * lr + li * li
    fr = (nr * lr + ni * li) / den
    fi = (ni * lr - nr * li) / den
    return ab_r, ab_i, fr, fi


def _prep_kernel(lr_a, li_a, ld_a, lr_b, li_b, ld_b, br_ref, bi_ref, ar_ref, ai_ref, bbr_ref, bbi_ref):
    ab_r, ab_i, _, _ = _zoh(lr_a[...], li_a[...], ld_a[...])
    ar_ref[...] = ab_r
    ai_ref[...] = ab_i
    _, _, fr, fi = _zoh(lr_b[...], li_b[...], ld_b[...])
    br, bi = br_ref[...], bi_ref[...]
    bbr_ref[...] = fr * br - fi * bi
    bbi_ref[...] = fr * bi + fi * br


def _kv_kernel(mem_ref, g_ref, wk_ref, wv_ref, kt_ref, v_ref):
    mn = _rmsnorm(mem_ref[...], g_ref[...]).astype(BF16)
    k = _dot(mn, wk_ref[...])
    kt_ref[...] = k.T.astype(BF16)
    v_ref[...] = _dot(mn, wv_ref[...]).astype(BF16)


def _mixer_kernel(x_ref, gpre_ref, win_ref, wb_ref, ar_ref, ai_ref, wcr_ref, wci_ref, dskip_ref,
                  gv_ref, ones_ref, wsp_ref, bias_ref, wout_ref, gpost_ref,
                  o_ref, st_ref, drv_ref, *, nb, tc):
    m = nb * tc
    nblk = tc // SUBLANES

    @pl.when(pl.program_id(0) == 0)
    def _():
        st_ref[...] = jnp.zeros_like(st_ref)

    x = x_ref[...].reshape(m, D_MODEL)
    h = _rmsnorm(x, gpre_ref[...]).astype(BF16)
    proj = _dot(h, win_ref[...])
    s_in = proj[:, 0:D_SSM]
    s_gate = proj[:, D_SSM:2 * D_SSM]
    g_u = proj[:, 2 * D_SSM:2 * D_SSM + D_GMLP]
    g_vv = proj[:, 2 * D_SSM + D_GMLP:]

    s_in_bf = s_in.astype(BF16)
    n_tiles = N_SLABS * LANES // MXU_DIM
    for j in range(n_tiles):
        ks = LANES * ((j % (n_tiles // 2)) // 2)
        bt = _dot(s_in_bf[:, ks:ks + LANES], wb_ref[ks:ks + LANES, MXU_DIM * j:MXU_DIM * (j + 1)])
        for k in range(2):
            s = 2 * j + k
            drv_ref[:, SUBLANES * s:SUBLANES * (s + 1), :] = (
                bt[:, LANES * k:LANES * (k + 1)].reshape(m // SUBLANES, SUBLANES, LANES))

    ar = (ar_ref[0:8, :], ar_ref[8:16, :])
    ai = (ai_ref[0:8, :], ai_ref[8:16, :])
    half = N_SLABS // 2 * SUBLANES

    def scan_block(tb, carry):
        carry = list(carry)
        for r in range(SUBLANES):
            for b in range(nb):
                row = b * nblk + tb
                for p in range(2):
                    xr, xi = carry[4 * b + p], carry[4 * b + 2 + p]
                    rr = pl.ds(64 * p + r, SUBLANES, stride=SUBLANES)
                    ri = pl.ds(half + 64 * p + r, SUBLANES, stride=SUBLANES)
                    nr = ar[p] * xr - ai[p] * xi + drv_ref[row, rr, :]
                    ni = ar[p] * xi + ai[p] * xr + drv_ref[row, ri, :]
                    drv_ref[row, rr, :] = nr
                    drv_ref[row, ri, :] = ni
                    carry[4 * b + p], carry[4 * b + 2 + p] = nr, ni
        return tuple(carry)

    init = tuple(st_ref[SUBLANES * i:SUBLANES * (i + 1), :] for i in range(4 * nb))
    fin = lax.fori_loop(0, nblk, scan_block, init)
    for i in range(4 * nb):
        st_ref[SUBLANES * i:SUBLANES * (i + 1), :] = fin[i]

    def slab_pair(s):
        parts = [drv_ref[:, SUBLANES * q:SUBLANES * (q + 1), :].reshape(m, LANES) for q in (s, s + 1)]
        return jnp.concatenate(parts, axis=1).astype(BF16)

    ys = []
    for n in range(D_SSM // MXU_DIM):
        accr = acci = None
        for q in range(4):
            sr = 8 * n + 2 * q
            rows = slice(LANES * sr, LANES * sr + MXU_DIM)
            cols = slice(MXU_DIM * n, MXU_DIM * (n + 1))
            dr = _dot(slab_pair(sr), wcr_ref[rows, cols])
            di = _dot(slab_pair(N_SLABS // 2 + sr), wci_ref[rows, cols])
            accr = dr if accr is None else accr + dr
            acci = di if acci is None else acci + di
        ys.append(accr - acci)
    y = jnp.concatenate(ys, axis=1) + dskip_ref[...] * s_in
    y_ssm = jax.nn.gelu(y) * jax.nn.sigmoid(s_gate)

    sq = g_vv * g_vv
    sq_hi = sq.astype(BF16)
    sq_lo = (sq - sq_hi.astype(F32)).astype(BF16)
    ssum = _dot(sq_hi, ones_ref[...]) + _dot(sq_lo, ones_ref[...])
    vn = g_vv * lax.rsqrt(ssum * (1.0 / GMLP_HEAD_DIM) + RMS_EPS) * gv_ref[...]
    lane = lax.broadcasted_iota(jnp.int32, (m, D_GMLP), 1) % LANES
    vn_lo = jnp.where(lane < GMLP_HEAD_DIM, vn, 0.0).astype(BF16)
    vn_hi = jnp.where(lane >= GMLP_HEAD_DIM, vn, 0.0).astype(BF16)
    rowi = lax.broadcasted_iota(jnp.int32, (CHUNK, 2 * CHUNK), 0)
    coli = lax.broadcasted_iota(jnp.int32, (CHUNK, 2 * CHUNK), 1) % CHUNK
    wsm = [jnp.where(rowi >= coli, wsp_ref[k], 0.0).astype(BF16) for k in range(GMLP_HEADS // 2)]
    chunks = []
    for c in range(m // CHUNK):
        rs = slice(CHUNK * c, CHUNK * (c + 1))
        outs = []
        for k in range(GMLP_HEADS // 2):
            cs = slice(LANES * k, LANES * (k + 1))
            rhs = jnp.concatenate([vn_lo[rs, cs], vn_hi[rs, cs]], axis=0)
            outs.append(_dot(wsm[k], rhs))
        chunks.append(jnp.concatenate(outs, axis=1) + bias_ref[...])
    y_gmlp = g_u * jnp.concatenate(chunks, axis=0)

    mix_in = jnp.concatenate([y_ssm, y_gmlp], axis=1).astype(BF16)
    mix = _dot(mix_in, wout_ref[...])
    o_ref[...] = (x + _rmsnorm(mix, gpost_ref[...])).reshape(nb, tc, D_MODEL)


def _xattn_kernel(x_ref, gpre_ref, wq_ref, kt_ref, v_ref, wo_ref, gpost_ref, o_ref, *scratch, n_split):
    h_ref, q_ref, att_ref, xa_ref = (scratch[i * n_split:(i + 1) * n_split] for i in range(4))
    rows = x_ref.shape[0] // n_split

    strips = range(0, rows, ATT_STRIP)

    def norm_in(part):
        for r0 in strips:
            x = x_ref[pl.ds(part * rows + r0, ATT_STRIP), :]
            h_ref[part][pl.ds(r0, ATT_STRIP), :] = _rmsnorm(x, gpre_ref[...]).astype(BF16)

    def query(part):
        q_ref[part][...] = (_dot(h_ref[part][...], wq_ref[...]) * (1.0 / math.sqrt(XATTN_HEAD_DIM))).astype(BF16)

    def attend(part):
        for hd in range(XATTN_HEADS):
            cs = slice(XATTN_HEAD_DIM * hd, XATTN_HEAD_DIM * (hd + 1))
            s = _dot(q_ref[part][:, cs], kt_ref[cs, :])
            ps = []
            for r0 in strips:
                ss = s[r0:r0 + ATT_STRIP, :]
                e = jnp.exp(ss - jnp.max(ss, axis=-1, keepdims=True))
                ps.append((e / jnp.sum(e, axis=-1, keepdims=True)).astype(BF16))
            att_ref[part][:, cs] = _dot(jnp.concatenate(ps, axis=0), v_ref[:, cs]).astype(BF16)

    def project(part):
        xa_ref[part][...] = _dot(att_ref[part][...], wo_ref[...])

    def finish(part):
        for r0 in strips:
            rs = pl.ds(part * rows + r0, ATT_STRIP)
            o_ref[rs, :] = x_ref[rs, :] + _rmsnorm(xa_ref[part][pl.ds(r0, ATT_STRIP), :], gpost_ref[...])

    norm_in(0)
    query(0)
    for part in range(n_split):
        if part + 1 < n_split:
            norm_in(part + 1)
            query(part + 1)
        attend(part)
        project(part)
        if part >= 1:
            finish(part - 1)
    finish(n_split - 1)


def _ffn_kernel(x_ref, gpre_ref, wup_ref, cw_ref, cb_ref, wdn_ref, gpost_ref, o_ref,
                carry_ref, h_ref, a0_ref, a1_ref, act0_ref, act1_ref, acc_ref, *, tc):
    @pl.when(pl.program_id(1) == 0)
    def _():
        carry_ref[...] = jnp.zeros_like(carry_ref)

    h_ref[...] = _rmsnorm(x_ref[...], gpre_ref[...]).astype(BF16)
    acc_ref[...] = jnp.zeros_like(acc_ref)
    pad = SUBLANES
    half = FF_TILE // LANES

    def up(f, a_ref):
        for part in range(2):
            col = pl.multiple_of(part * D_FF + f * FF_TILE, FF_TILE)
            res = _dot(h_ref[...], wup_ref[:, pl.ds(col, FF_TILE)])
            for kk in range(half):
                k = part * half + kk
                a_ref[k, 0:pad, :] = carry_ref[f, k]
                a_ref[k, pad:pad + tc, :] = res[:, kk * LANES:(kk + 1) * LANES]
                carry_ref[f, k] = a_ref[k, tc:tc + pad, :]

    def conv_act(f, a_ref, act_ref):
        for kk in range(half):
            slabs = (kk, half + kk)
            scale = (0.5, 1.0)
            taps = [[s * jnp.broadcast_to(cw_ref[f, j:j + 1, k * LANES:(k + 1) * LANES], (SUBLANES, LANES))
                     for j in range(3)] for k, s in zip(slabs, scale)]
            bias = [s * jnp.broadcast_to(cb_ref[f, :, k * LANES:(k + 1) * LANES], (SUBLANES, LANES))
                    for k, s in zip(slabs, scale)]
            for r0 in range(0, tc, FFN_STRIP):
                cs = [[], []]
                for r in range(r0, r0 + FFN_STRIP, SUBLANES):
                    for i, k in enumerate(slabs):
                        a2 = a_ref[k, pl.ds(pad + r - 2, SUBLANES), :]
                        a1 = a_ref[k, pl.ds(pad + r - 1, SUBLANES), :]
                        a0 = a_ref[k, pl.ds(pad + r, SUBLANES), :]
                        cs[i].append(taps[i][0] * a2 + taps[i][1] * a1 + taps[i][2] * a0 + bias[i])
                val = jnp.concatenate(cs[0], axis=0)
                gate = jnp.concatenate(cs[1], axis=0)
                th = jnp.tanh(gate * (GELU_K0 + GELU_K1 * (gate * gate)))
                act_ref[pl.ds(r0, FFN_STRIP), kk * LANES:(kk + 1) * LANES] = (
                    (val * gate) * (th + 1.0)).astype(BF16)

    def down(f, act_ref):
        acc_ref[...] += _dot(act_ref[...], wdn_ref[f])

    def step(f, a_cur, act_cur, a_nxt, act_prv):
        up(f + 1, a_nxt)
        down(f - 1, act_prv)
        conv_act(f, a_cur, act_cur)

    up(0, a0_ref)
    up(1, a1_ref)
    conv_act(0, a0_ref, act0_ref)

    def pair(k, c):
        f = 2 * k + 1
        step(f, a1_ref, act1_ref, a0_ref, act0_ref)
        step(f + 1, a0_ref, act0_ref, a1_ref, act1_ref)
        return c

    n_pairs = (N_FF_TILES - 2) // 2
    lax.fori_loop(0, n_pairs, pair, 0)
    step(N_FF_TILES - 2, a1_ref, act1_ref, a0_ref, act0_ref)
    conv_act(N_FF_TILES - 1, a0_ref, act0_ref)
    down(N_FF_TILES - 2, act1_ref)
    down(N_FF_TILES - 1, act0_ref)
    o_ref[...] = x_ref[...] + _rmsnorm(acc_ref[...], gpost_ref[...])


def kernel(x, mem, g_mix_pre, w_in, lam_re, lam_im, log_dt, b_re, b_im, c_re, c_im, d_skip, g_v, w_s, b_s, w_out, g_mix_post, g_x_pre, g_mem, w_q, w_k, w_v, w_o, g_x_post, g_ffn_pre, w_up, conv_w, conv_b, w_down, g_ffn_post):
    bsz, seq, d = x.shape
    depth = w_in.shape[0]
    G, P, C = N_SSM_GROUPS, SSM_STATE, SSM_GROUP
    assert d == D_MODEL and seq % FFN_TC == 0 and seq % MIX_TC == 0 and mem.shape[1] == MEM_LEN
    assert seq % ATT_TC == 0 and ATT_TC % (ATT_SPLIT * ATT_STRIP) == 0
    assert N_FF_TILES % 2 == 1 and N_FF_TILES >= 3

    a_shape = (depth, G * P // LANES, LANES)
    ld_full = jnp.broadcast_to(log_dt[:, :, None], (depth, G, P))
    rep = lambda t: jnp.broadcast_to(t[:, :, None, :], (depth, G, C, P)).reshape(depth, G * C, P)
    gcp = lambda t: jnp.transpose(t, (0, 1, 3, 2)).reshape(depth, G * C, P)
    ar, ai, bbr, bbi = pl.pallas_call(
        _prep_kernel,
        out_shape=[jax.ShapeDtypeStruct(a_shape, F32)] * 2
        + [jax.ShapeDtypeStruct((depth, G * C, P), F32)] * 2,
        name="ssm_prep",
    )(lam_re.reshape(a_shape), lam_im.reshape(a_shape), ld_full.reshape(a_shape),
      rep(lam_re), rep(lam_im), rep(ld_full), gcp(b_re), gcp(b_im))

    def block_diag(rows, n_in, n_out):
        r = lax.broadcasted_iota(jnp.int32, (G * n_in, G * n_out), 0) // n_in
        c = lax.broadcasted_iota(jnp.int32, (G * n_in, G * n_out), 1) // n_out
        return jnp.where(r == c, jnp.tile(rows, (1, 1, G)), 0.0).astype(BF16)

    gpc = lambda t: jnp.transpose(t, (0, 1, 3, 2)).reshape(depth, G * P, C)
    wb = jnp.concatenate([block_diag(bbr, C, P), block_diag(bbi, C, P)], axis=2)
    wcr = block_diag(gpc(c_re), P, C)
    wci = block_diag(gpc(c_im), P, C)
    dsk = d_skip.reshape(depth, 1, D_SSM)

    vec = lambda t: t.reshape(depth, 1, -1)
    ones_bd = jnp.kron(jnp.eye(GMLP_HEADS, dtype=F32), jnp.ones((GMLP_HEAD_DIM, GMLP_HEAD_DIM), F32)).astype(BF16)
    wsp = (w_s.reshape(depth, GMLP_HEADS // 2, 2, CHUNK, CHUNK).transpose(0, 1, 3, 2, 4)
           .reshape(depth, GMLP_HEADS // 2, CHUNK, 2 * CHUNK))
    bias2d = jnp.repeat(jnp.transpose(b_s, (0, 2, 1)), GMLP_HEAD_DIM, axis=2)

    w_in_b, w_out_b = w_in.astype(BF16), w_out.astype(BF16)
    w_q_b, w_k_b, w_v_b, w_o_b = (t.astype(BF16) for t in (w_q, w_k, w_v, w_o))
    ff_tiles = lambda t: (t.reshape(t.shape[:-1] + (2, N_FF_TILES, FF_TILE))
                          .swapaxes(-3, -2).reshape(t.shape[:-1] + (N_FF_TILES, 2 * FF_TILE)))
    w_up_b = w_up.astype(BF16)
    cw_t = jnp.moveaxis(ff_tiles(conv_w), 2, 1)
    cb_t = ff_tiles(conv_b)[:, :, None, :]
    w_dn_t = w_down.astype(BF16).reshape(depth, N_FF_TILES, FF_TILE, D_MODEL)

    kt, vv = pl.pallas_call(
        _kv_kernel,
        grid=(depth, bsz),
        in_specs=[
            pl.BlockSpec((None, MEM_LEN, D_MODEL), lambda l, b: (b, 0, 0)),
            pl.BlockSpec((None, 1, D_MODEL), lambda l, b: (l, 0, 0)),
            pl.BlockSpec((None, D_MODEL, D_MODEL), lambda l, b: (l, 0, 0)),
            pl.BlockSpec((None, D_MODEL, D_MODEL), lambda l, b: (l, 0, 0)),
        ],
        out_specs=[
            pl.BlockSpec((None, None, D_MODEL, MEM_LEN), lambda l, b: (l, b, 0, 0)),
            pl.BlockSpec((None, None, MEM_LEN, D_MODEL), lambda l, b: (l, b, 0, 0)),
        ],
        out_shape=[jax.ShapeDtypeStruct((depth, bsz, D_MODEL, MEM_LEN), BF16),
                   jax.ShapeDtypeStruct((depth, bsz, MEM_LEN, D_MODEL), BF16)],
        compiler_params=pltpu.CompilerParams(dimension_semantics=("arbitrary", "arbitrary"),
                                             vmem_limit_bytes=VMEM_LIMIT),
        name="mem_kv",
    )(mem, vec(g_mem), w_k_b, w_v_b)

    x_shape = jax.ShapeDtypeStruct(x.shape, F32)
    for l in range(depth):
        cs = functools.partial(_const_spec, layer=l)
        x = pl.pallas_call(
            functools.partial(_mixer_kernel, nb=bsz, tc=MIX_TC),
            grid=(seq // MIX_TC,),
            in_specs=[
                pl.BlockSpec((bsz, MIX_TC, D_MODEL), lambda t: (0, t, 0)),
                cs((1, D_MODEL)), cs((D_MODEL, 2 * D_SSM + 2 * D_GMLP)), cs((D_SSM, N_SLABS * LANES)),
                cs(a_shape[1:]), cs(a_shape[1:]), cs((G * P, D_SSM)), cs((G * P, D_SSM)), cs((1, D_SSM)),
                cs((1, D_GMLP)),
                pl.BlockSpec((D_GMLP, D_GMLP), lambda t: (0, 0), pipeline_mode=pl.Buffered(1)),
                cs((GMLP_HEADS // 2, CHUNK, 2 * CHUNK)), cs((CHUNK, D_GMLP)),
                cs((D_MODEL, D_MODEL)), cs((1, D_MODEL)),
            ],
            out_specs=pl.BlockSpec((bsz, MIX_TC, D_MODEL), lambda t: (0, t, 0)),
            out_shape=x_shape,
            scratch_shapes=[
                pltpu.VMEM((4 * bsz * SUBLANES, LANES), F32),
                pltpu.VMEM((bsz * MIX_TC // SUBLANES, N_SLABS * SUBLANES, LANES), F32),
            ],
            compiler_params=pltpu.CompilerParams(dimension_semantics=("arbitrary",),
                                                 vmem_limit_bytes=VMEM_LIMIT),
            name="mixer",
        )(x, vec(g_mix_pre), w_in_b, wb, ar, ai, wcr, wci, dsk, vec(g_v), ones_bd, wsp, bias2d,
          w_out_b, vec(g_mix_post))

        x = pl.pallas_call(
            functools.partial(_xattn_kernel, n_split=ATT_SPLIT),
            grid=(bsz, seq // ATT_TC),
            in_specs=[
                pl.BlockSpec((None, ATT_TC, D_MODEL), lambda b, t: (b, t, 0)),
                cs((1, D_MODEL)), cs((D_MODEL, D_MODEL)),
                pl.BlockSpec((None, None, D_MODEL, MEM_LEN), lambda b, t: (l, b, 0, 0)),
                pl.BlockSpec((None, None, MEM_LEN, D_MODEL), lambda b, t: (l, b, 0, 0)),
                cs((D_MODEL, D_MODEL)), cs((1, D_MODEL)),
            ],
            out_specs=pl.BlockSpec((None, ATT_TC, D_MODEL), lambda b, t: (b, t, 0)),
            out_shape=x_shape,
            scratch_shapes=[pltpu.VMEM((ATT_TC // ATT_SPLIT, D_MODEL), dt)
                            for dt in (BF16, BF16, BF16, F32) for _ in range(ATT_SPLIT)],
            compiler_params=pltpu.CompilerParams(dimension_semantics=("arbitrary", "arbitrary"),
                                                 vmem_limit_bytes=VMEM_LIMIT),
            name="xattn",
        )(x, vec(g_x_pre), w_q_b, kt, vv, w_o_b, vec(g_x_post))

        x = pl.pallas_call(
            functools.partial(_ffn_kernel, tc=FFN_TC),
            grid=(bsz, seq // FFN_TC),
            in_specs=[
                pl.BlockSpec((None, FFN_TC, D_MODEL), lambda b, t: (b, t, 0)),
                cs((1, D_MODEL)), cs((D_MODEL, 2 * D_FF)), cs((N_FF_TILES, 3, 2 * FF_TILE)),
                cs((N_FF_TILES, 1, 2 * FF_TILE)), cs((N_FF_TILES, FF_TILE, D_MODEL)), cs((1, D_MODEL)),
            ],
            out_specs=pl.BlockSpec((None, FFN_TC, D_MODEL), lambda b, t: (b, t, 0)),
            out_shape=x_shape,
            scratch_shapes=[
                pltpu.VMEM((N_FF_TILES, 2 * FF_TILE // LANES, SUBLANES, LANES), F32),
                pltpu.VMEM((FFN_TC, D_MODEL), BF16),
                pltpu.VMEM((2 * FF_TILE // LANES, SUBLANES + FFN_TC, LANES), F32),
                pltpu.VMEM((2 * FF_TILE // LANES, SUBLANES + FFN_TC, LANES), F32),
                pltpu.VMEM((FFN_TC, FF_TILE), BF16),
                pltpu.VMEM((FFN_TC, FF_TILE), BF16),
                pltpu.VMEM((FFN_TC, D_MODEL), F32),
            ],
            compiler_params=pltpu.CompilerParams(dimension_semantics=("arbitrary", "arbitrary"),
                                                 vmem_limit_bytes=VMEM_LIMIT),
            name="convffn",
        )(x, vec(g_ffn_pre), w_up_b, cw_t, cb_t, w_dn_t, vec(g_ffn_post))
    return x
```

```python
import functools
import math

import jax
import jax.numpy as jnp
from jax import lax
from jax.experimental import pallas as pl
from jax.experimental.pallas import tpu as pltpu

F32 = jnp.float32
BF16 = jnp.bfloat16

D_MODEL = 1024
D_SSM = 512
SSM_GROUP = 16
N_SSM_GROUPS = 32
SSM_STATE = 64
D_GMLP = 512
GMLP_HEADS = 8
GMLP_HEAD_DIM = 64
CHUNK = 128
MEM_LEN = 256
XATTN_HEADS = 4
XATTN_HEAD_DIM = 256
D_FF = 2816
RMS_EPS = 1e-6
GELU_K0 = math.sqrt(2.0 / math.pi)
GELU_K1 = GELU_K0 * 0.044715

LANES = 128
SUBLANES = 8
MXU_DIM = 256
N_SLABS = 2 * N_SSM_GROUPS * SSM_STATE // LANES
VMEM_LIMIT = 56 * 1024 * 1024

MIX_TC = 256
ATT_TC = 1024
ATT_SPLIT = 2
ATT_STRIP = 16
FFN_TC = 512
FF_TILE = MXU_DIM
FFN_STRIP = 16
N_FF_TILES = D_FF // FF_TILE


def _rmsnorm(xf, g):
    ms = jnp.mean(xf * xf, axis=-1, keepdims=True)
    return xf * lax.rsqrt(ms + RMS_EPS) * g


def _dot(a, b):
    return jnp.dot(a, b, preferred_element_type=F32)


def _const_spec(shape, layer):
    nd = len(shape)
    return pl.BlockSpec((None,) + tuple(shape), lambda *_: (layer,) + (0,) * nd,
                        pipeline_mode=pl.Buffered(1))


def _zoh(lr, li, log_dt):
    dt = jnp.exp(log_dt)
    mag = jnp.exp(lr * dt)
    ab_r = mag * jnp.cos(li * dt)
    ab_i = mag * jnp.sin(li * dt)
    nr, ni = ab_r - 1.0, ab_i
    den = lr * lr + li * li
    fr = (nr * lr + ni * li) / den
    fi = (ni * lr - nr * li) / den
    return ab_r, ab_i, fr, fi


def _prep_kernel(lr_ref, li_ref, ld_ref, br_ref, bi_ref, ar_ref, ai_ref, bbr_ref, bbi_ref):
    depth, groups = lr_ref.shape[0], lr_ref.shape[1]
    for l in range(depth):
        ab_r, ab_i, fr, fi = _zoh(lr_ref[l], li_ref[l], ld_ref[l])
        ar_ref[l] = ab_r
        ai_ref[l] = ab_i
        for g in range(groups):
            br, bi = br_ref[l, g], bi_ref[l, g]
            bbr_ref[l, g] = fr[g:g + 1, :] * br - fi[g:g + 1, :] * bi
            bbi_ref[l, g] = fr[g:g + 1, :] * bi + fi[g:g + 1, :] * br


def _kv_kernel(mem_ref, g_ref, wk_ref, wv_ref, kt_ref, v_ref):
    mn = _rmsnorm(mem_ref[...], g_ref[...]).astype(BF16)
    k = _dot(mn, wk_ref[...])
    kt_ref[...] = k.T.astype(BF16)
    v_ref[...] = _dot(mn, wv_ref[...]).astype(BF16)


def _mixer_kernel(x_ref, gpre_ref, win_ref, wb_ref, ar_ref, ai_ref, wcr_ref, wci_ref, dskip_ref,
                  gv_ref, ones_ref, wsp_ref, bias_ref, wout_ref, gpost_ref,
                  o_ref, st_ref, drv_ref, *, nb, tc):
    m = nb * tc
    nblk = tc // SUBLANES

    @pl.when(pl.program_id(0) == 0)
    def _():
        st_ref[...] = jnp.zeros_like(st_ref)

    x = x_ref[...].reshape(m, D_MODEL)
    h = _rmsnorm(x, gpre_ref[...]).astype(BF16)
    proj = _dot(h, win_ref[...])
    s_in = proj[:, 0:D_SSM]
    s_gate = proj[:, D_SSM:2 * D_SSM]
    g_u = proj[:, 2 * D_SSM:2 * D_SSM + D_GMLP]
    g_vv = proj[:, 2 * D_SSM + D_GMLP:]

    ssum = _dot((g_vv * g_vv).astype(BF16), ones_ref[...])
    vn = g_vv * lax.rsqrt(ssum * (1.0 / GMLP_HEAD_DIM) + RMS_EPS) * gv_ref[...]
    lane = lax.broadcasted_iota(jnp.int32, (m, D_GMLP), 1) % LANES
    vn_lo = jnp.where(lane < GMLP_HEAD_DIM, vn, 0.0).astype(BF16)
    vn_hi = jnp.where(lane >= GMLP_HEAD_DIM, vn, 0.0).astype(BF16)
    rowi = lax.broadcasted_iota(jnp.int32, (CHUNK, 2 * CHUNK), 0)
    coli = lax.broadcasted_iota(jnp.int32, (CHUNK, 2 * CHUNK), 1) % CHUNK
    wsm = [jnp.where(rowi >= coli, wsp_ref[k], 0.0).astype(BF16) for k in range(GMLP_HEADS // 2)]
    chunks = []
    for c in range(m // CHUNK):
        rs = slice(CHUNK * c, CHUNK * (c + 1))
        outs = []
        for k in range(GMLP_HEADS // 2):
            cs = slice(LANES * k, LANES * (k + 1))
            rhs = jnp.concatenate([vn_lo[rs, cs], vn_hi[rs, cs]], axis=0)
            outs.append(_dot(wsm[k], rhs))
        chunks.append(jnp.concatenate(outs, axis=1) + bias_ref[...])
    y_gmlp = g_u * jnp.concatenate(chunks, axis=0)

    s_in_bf = s_in.astype(BF16)
    n_tiles = N_SLABS * LANES // MXU_DIM
    for j in range(n_tiles):
        ks = LANES * ((j % (n_tiles // 2)) // 2)
        bt = _dot(s_in_bf[:, ks:ks + LANES], wb_ref[ks:ks + LANES, MXU_DIM * j:MXU_DIM * (j + 1)])
        for k in range(2):
            s = 2 * j + k
            drv_ref[:, SUBLANES * s:SUBLANES * (s + 1), :] = (
                bt[:, LANES * k:LANES * (k + 1)].reshape(m // SUBLANES, SUBLANES, LANES))

    ar = (ar_ref[0:8, :], ar_ref[8:16, :])
    ai = (ai_ref[0:8, :], ai_ref[8:16, :])
    half = N_SLABS // 2 * SUBLANES

    def scan_block(tb, carry):
        carry = list(carry)
        for r in range(SUBLANES):
            for b in range(nb):
                row = b * nblk + tb
                for p in range(2):
                    xr, xi = carry[4 * b + p], carry[4 * b + 2 + p]
                    rr = pl.ds(64 * p + r, SUBLANES, stride=SUBLANES)
                    ri = pl.ds(half + 64 * p + r, SUBLANES, stride=SUBLANES)
                    nr = ar[p] * xr - ai[p] * xi + drv_ref[row, rr, :]
                    ni = ar[p] * xi + ai[p] * xr + drv_ref[row, ri, :]
                    drv_ref[row, rr, :] = nr
                    drv_ref[row, ri, :] = ni
                    carry[4 * b + p], carry[4 * b + 2 + p] = nr, ni
        return tuple(carry)

    init = tuple(st_ref[SUBLANES * i:SUBLANES * (i + 1), :] for i in range(4 * nb))
    fin = lax.fori_loop(0, nblk, scan_block, init)
    for i in range(4 * nb):
        st_ref[SUBLANES * i:SUBLANES * (i + 1), :] = fin[i]

    def slab_pair(s):
        parts = [drv_ref[:, SUBLANES * q:SUBLANES * (q + 1), :].reshape(m, LANES) for q in (s, s + 1)]
        return jnp.concatenate(parts, axis=1).astype(BF16)

    ys = []
    for n in range(D_SSM // MXU_DIM):
        accr = acci = None
        for q in range(4):
            sr = 8 * n + 2 * q
            rows = slice(LANES * sr, LANES * sr + MXU_DIM)
            cols = slice(MXU_DIM * n, MXU_DIM * (n + 1))
            dr = _dot(slab_pair(sr), wcr_ref[rows, cols])
            di = _dot(slab_pair(N_SLABS // 2 + sr), wci_ref[rows, cols])
            accr = dr if accr is None else accr + dr
            acci = di if acci is None else acci + di
        ys.append(accr - acci)
    y = jnp.concatenate(ys, axis=1) + dskip_ref[...] * s_in
    y_ssm = jax.nn.gelu(y) * jax.nn.sigmoid(s_gate)

    mix_in = jnp.concatenate([y_ssm, y_gmlp], axis=1).astype(BF16)
    mix = _dot(mix_in, wout_ref[...])
    o_ref[...] = (x + _rmsnorm(mix, gpost_ref[...])).reshape(nb, tc, D_MODEL)


def _xattn_kernel(x_ref, gpre_ref, wq_ref, kt_ref, v_ref, wo_ref, gpost_ref, o_ref, *scratch, n_split):
    h_ref, q_ref, att_ref, xa_ref = (scratch[i * n_split:(i + 1) * n_split] for i in range(4))
    rows = x_ref.shape[0] // n_split

    strips = range(0, rows, ATT_STRIP)

    def norm_in(part):
        for r0 in strips:
            x = x_ref[pl.ds(part * rows + r0, ATT_STRIP), :]
            h_ref[part][pl.ds(r0, ATT_STRIP), :] = _rmsnorm(x, gpre_ref[...]).astype(BF16)

    def query(part):
        q_ref[part][...] = (_dot(h_ref[part][...], wq_ref[...]) * (1.0 / math.sqrt(XATTN_HEAD_DIM))).astype(BF16)

    def attend(part):
        for hd in range(XATTN_HEADS):
            cs = slice(XATTN_HEAD_DIM * hd, XATTN_HEAD_DIM * (hd + 1))
            s = _dot(q_ref[part][:, cs], kt_ref[cs, :])
            ps = []
            for r0 in strips:
                ss = s[r0:r0 + ATT_STRIP, :]
                e = jnp.exp(ss - jnp.max(ss, axis=-1, keepdims=True))
                ps.append((e / jnp.sum(e, axis=-1, keepdims=True)).astype(BF16))
            att_ref[part][:, cs] = _dot(jnp.concatenate(ps, axis=0), v_ref[:, cs]).astype(BF16)

    def project(part):
        xa_ref[part][...] = _dot(att_ref[part][...], wo_ref[...])

    def finish(part):
        for r0 in strips:
            rs = pl.ds(part * rows + r0, ATT_STRIP)
            o_ref[rs, :] = x_ref[rs, :] + _rmsnorm(xa_ref[part][pl.ds(r0, ATT_STRIP), :], gpost_ref[...])

    norm_in(0)
    query(0)
    for part in range(n_split):
        if part + 1 < n_split:
            norm_in(part + 1)
            query(part + 1)
        attend(part)
        project(part)
        if part >= 1:
            finish(part - 1)
    finish(n_split - 1)


def _ffn_kernel(x_ref, gpre_ref, wup_ref, cw_ref, cb_ref, wdn_ref, gpost_ref, o_ref,
                carry_ref, h_ref, a0_ref, a1_ref, act0_ref, act1_ref, acc_ref, *, tc):
    @pl.when(pl.program_id(1) == 0)
    def _():
        carry_ref[...] = jnp.zeros_like(carry_ref)

    h_ref[...] = _rmsnorm(x_ref[...], gpre_ref[...]).astype(BF16)
    acc_ref[...] = jnp.zeros_like(acc_ref)
    pad = SUBLANES
    half = FF_TILE // LANES

    def up(f, a_ref):
        for part in range(2):
            col = pl.multiple_of(part * D_FF + f * FF_TILE, FF_TILE)
            res = _dot(h_ref[...], wup_ref[:, pl.ds(col, FF_TILE)])
            for kk in range(half):
                k = part * half + kk
                a_ref[k, 0:pad, :] = carry_ref[f, k]
                a_ref[k, pad:pad + tc, :] = res[:, kk * LANES:(kk + 1) * LANES]
                carry_ref[f, k] = a_ref[k, tc:tc + pad, :]

    def conv_act(f, a_ref, act_ref):
        for kk in range(half):
            slabs = (kk, half + kk)
            scale = (0.5, 1.0)
            taps = [[s * jnp.broadcast_to(cw_ref[f, j:j + 1, k * LANES:(k + 1) * LANES], (SUBLANES, LANES))
                     for j in range(3)] for k, s in zip(slabs, scale)]
            bias = [s * jnp.broadcast_to(cb_ref[f, :, k * LANES:(k + 1) * LANES], (SUBLANES, LANES))
                    for k, s in zip(slabs, scale)]
            for r0 in range(0, tc, FFN_STRIP):
                cs = [[], []]
                for r in range(r0, r0 + FFN_STRIP, SUBLANES):
                    for i, k in enumerate(slabs):
                        a2 = a_ref[k, pl.ds(pad + r - 2, SUBLANES), :]
                        a1 = a_ref[k, pl.ds(pad + r - 1, SUBLANES), :]
                        a0 = a_ref[k, pl.ds(pad + r, SUBLANES), :]
                        cs[i].append(taps[i][0] * a2 + taps[i][1] * a1 + taps[i][2] * a0 + bias[i])
                val = jnp.concatenate(cs[0], axis=0)
                gate = jnp.concatenate(cs[1], axis=0)
                th = jnp.tanh(gate * (GELU_K0 + GELU_K1 * (gate * gate)))
                act_ref[pl.ds(r0, FFN_STRIP), kk * LANES:(kk + 1) * LANES] = (
                    (val * gate) * (th + 1.0)).astype(BF16)

    def down(f, act_ref):
        acc_ref[...] += _dot(act_ref[...], wdn_ref[f])

    def step(f, a_cur, act_cur, a_nxt, act_prv):
        up(f + 1, a_nxt)
        down(f - 1, act_prv)
        conv_act(f, a_cur, act_cur)

    up(0, a0_ref)
    up(1, a1_ref)
    conv_act(0, a0_ref, act0_ref)

    def pair(k, c):
        f = 2 * k + 1
        step(f, a1_ref, act1_ref, a0_ref, act0_ref)
        step(f + 1, a0_ref, act0_ref, a1_ref, act1_ref)
        return c

    n_pairs = (N_FF_TILES - 2) // 2
    lax.fori_loop(0, n_pairs, pair, 0)
    step(N_FF_TILES - 2, a1_ref, act1_ref, a0_ref, act0_ref)
    conv_act(N_FF_TILES - 1, a0_ref, act0_ref)
    down(N_FF_TILES - 2, act1_ref)
    down(N_FF_TILES - 1, act0_ref)
    o_ref[...] = x_ref[...] + _rmsnorm(acc_ref[...], gpost_ref[...])


def kernel(x, mem, g_mix_pre, w_in, lam_re, lam_im, log_dt, b_re, b_im, c_re, c_im, d_skip, g_v, w_s, b_s, w_out, g_mix_post, g_x_pre, g_mem, w_q, w_k, w_v, w_o, g_x_post, g_ffn_pre, w_up, conv_w, conv_b, w_down, g_ffn_post):
    bsz, seq, d = x.shape
    depth = w_in.shape[0]
    G, P, C = N_SSM_GROUPS, SSM_STATE, SSM_GROUP
    assert d == D_MODEL and seq % FFN_TC == 0 and seq % MIX_TC == 0 and mem.shape[1] == MEM_LEN
    assert seq % ATT_TC == 0 and ATT_TC % (ATT_SPLIT * ATT_STRIP) == 0
    assert N_FF_TILES % 2 == 1 and N_FF_TILES >= 3

    a_shape = (depth, G * P // LANES, LANES)
    gcp = lambda t: jnp.transpose(t, (0, 1, 3, 2))
    ar, ai, bbr, bbi = pl.pallas_call(
        _prep_kernel,
        out_shape=[jax.ShapeDtypeStruct((depth, G, P), F32)] * 2
        + [jax.ShapeDtypeStruct((depth, G, C, P), F32)] * 2,
        name="ssm_prep",
    )(lam_re, lam_im, log_dt[:, :, None], gcp(b_re), gcp(b_im))
    ar, ai = ar.reshape(a_shape), ai.reshape(a_shape)
    bbr, bbi = bbr.reshape(depth, G * C, P), bbi.reshape(depth, G * C, P)

    def block_diag(rows, n_in, n_out):
        r = lax.broadcasted_iota(jnp.int32, (G * n_in, G * n_out), 0) // n_in
        c = lax.broadcasted_iota(jnp.int32, (G * n_in, G * n_out), 1) // n_out
        return jnp.where(r == c, jnp.tile(rows, (1, 1, G)), 0.0).astype(BF16)

    gpc = lambda t: jnp.transpose(t, (0, 1, 3, 2)).reshape(depth, G * P, C)
    wb = jnp.concatenate([block_diag(bbr, C, P), block_diag(bbi, C, P)], axis=2)
    wcr = block_diag(gpc(c_re), P, C)
    wci = block_diag(gpc(c_im), P, C)
    dsk = d_skip.reshape(depth, 1, D_SSM)

    vec = lambda t: t.reshape(depth, 1, -1)
    ones_bd = jnp.kron(jnp.eye(GMLP_HEADS, dtype=F32), jnp.ones((GMLP_HEAD_DIM, GMLP_HEAD_DIM), F32)).astype(BF16)
    wsp = (w_s.reshape(depth, GMLP_HEADS // 2, 2, CHUNK, CHUNK).transpose(0, 1, 3, 2, 4)
           .reshape(depth, GMLP_HEADS // 2, CHUNK, 2 * CHUNK))
    bias2d = jnp.repeat(jnp.transpose(b_s, (0, 2, 1)), GMLP_HEAD_DIM, axis=2)

    w_in_b, w_out_b = w_in.astype(BF16), w_out.astype(BF16)
    w_q_b, w_k_b, w_v_b, w_o_b = (t.astype(BF16) for t in (w_q, w_k, w_v, w_o))
    ff_tiles = lambda t: (t.reshape(t.shape[:-1] + (2, N_FF_TILES, FF_TILE))
                          .swapaxes(-3, -2).reshape(t.shape[:-1] + (N_FF_TILES, 2 * FF_TILE)))
    w_up_b = w_up.astype(BF16)
    cw_t = jnp.moveaxis(ff_tiles(conv_w), 2, 1)
    cb_t = ff_tiles(conv_b)[:, :, None, :]
    w_dn_t = w_down.astype(BF16).reshape(depth, N_FF_TILES, FF_TILE, D_MODEL)

    kt, vv = pl.pallas_call(
        _kv_kernel,
        grid=(depth, bsz),
        in_specs=[
            pl.BlockSpec((None, MEM_LEN, D_MODEL), lambda l, b: (b, 0, 0)),
            pl.BlockSpec((None, 1, D_MODEL), lambda l, b: (l, 0, 0)),
            pl.BlockSpec((None, D_MODEL, D_MODEL), lambda l, b: (l, 0, 0)),
            pl.BlockSpec((None, D_MODEL, D_MODEL), lambda l, b: (l, 0, 0)),
        ],
        out_specs=[
            pl.BlockSpec((None, None, D_MODEL, MEM_LEN), lambda l, b: (l, b, 0, 0)),
            pl.BlockSpec((None, None, MEM_LEN, D_MODEL), lambda l, b: (l, b, 0, 0)),
        ],
        out_shape=[jax.ShapeDtypeStruct((depth, bsz, D_MODEL, MEM_LEN), BF16),
                   jax.ShapeDtypeStruct((depth, bsz, MEM_LEN, D_MODEL), BF16)],
        compiler_params=pltpu.CompilerParams(dimension_semantics=("arbitrary", "arbitrary"),
                                             vmem_limit_bytes=VMEM_LIMIT),
        name="mem_kv",
    )(mem, vec(g_mem), w_k_b, w_v_b)

    x_shape = jax.ShapeDtypeStruct(x.shape, F32)
    for l in range(depth):
        cs = functools.partial(_const_spec, layer=l)
        x = pl.pallas_call(
            functools.partial(_mixer_kernel, nb=bsz, tc=MIX_TC),
            grid=(seq // MIX_TC,),
            in_specs=[
                pl.BlockSpec((bsz, MIX_TC, D_MODEL), lambda t: (0, t, 0)),
                cs((1, D_MODEL)), cs((D_MODEL, 2 * D_SSM + 2 * D_GMLP)), cs((D_SSM, N_SLABS * LANES)),
                cs(a_shape[1:]), cs(a_shape[1:]), cs((G * P, D_SSM)), cs((G * P, D_SSM)), cs((1, D_SSM)),
                cs((1, D_GMLP)),
                pl.BlockSpec((D_GMLP, D_GMLP), lambda t: (0, 0), pipeline_mode=pl.Buffered(1)),
                cs((GMLP_HEADS // 2, CHUNK, 2 * CHUNK)), cs((CHUNK, D_GMLP)),
                cs((D_MODEL, D_MODEL)), cs((1, D_MODEL)),
            ],
            out_specs=pl.BlockSpec((bsz, MIX_TC, D_MODEL), lambda t: (0, t, 0)),
            out_shape=x_shape,
            scratch_shapes=[
                pltpu.VMEM((4 * bsz * SUBLANES, LANES), F32),
                pltpu.VMEM((bsz * MIX_TC // SUBLANES, N_SLABS * SUBLANES, LANES), F32),
            ],
            compiler_params=pltpu.CompilerParams(dimension_semantics=("arbitrary",),
                                                 vmem_limit_bytes=VMEM_LIMIT),
            name="mixer",
        )(x, vec(g_mix_pre), w_in_b, wb, ar, ai, wcr, wci, dsk, vec(g_v), ones_bd, wsp, bias2d,
          w_out_b, vec(g_mix_post))

        x = pl.pallas_call(
            functools.partial(_xattn_kernel, n_split=ATT_SPLIT),
            grid=(bsz, seq // ATT_TC),
            in_specs=[
                pl.BlockSpec((None, ATT_TC, D_MODEL), lambda b, t: (b, t, 0)),
                cs((1, D_MODEL)), cs((D_MODEL, D_MODEL)),
                pl.BlockSpec((None, None, D_MODEL, MEM_LEN), lambda b, t: (l, b, 0, 0)),
                pl.BlockSpec((None, None, MEM_LEN, D_MODEL), lambda b, t: (l, b, 0, 0)),
                cs((D_MODEL, D_MODEL)), cs((1, D_MODEL)),
            ],
            out_specs=pl.BlockSpec((None, ATT_TC, D_MODEL), lambda b, t: (b, t, 0)),
            out_shape=x_shape,
            scratch_shapes=[pltpu.VMEM((ATT_TC // ATT_SPLIT, D_MODEL), dt)
                            for dt in (BF16, BF16, BF16, F32) for _ in range(ATT_SPLIT)],
            compiler_params=pltpu.CompilerParams(dimension_semantics=("arbitrary", "arbitrary"),
                                                 vmem_limit_bytes=VMEM_LIMIT),
            name="xattn",
        )(x, vec(g_x_pre), w_q_b, kt, vv, w_o_b, vec(g_x_post))

        x = pl.pallas_call(
            functools.partial(_ffn_kernel, tc=FFN_TC),
            grid=(bsz, seq // FFN_TC),
            in_specs=[
                pl.BlockSpec((None, FFN_TC, D_MODEL), lambda b, t: (b, t, 0)),
                cs((1, D_MODEL)), cs((D_MODEL, 2 * D_FF)), cs((N_FF_TILES, 3, 2 * FF_TILE)),
                cs((N_FF_TILES, 1, 2 * FF_TILE)), cs((N_FF_TILES, FF_TILE, D_MODEL)), cs((1, D_MODEL)),
            ],
            out_specs=pl.BlockSpec((None, FFN_TC, D_MODEL), lambda b, t: (b, t, 0)),
            out_shape=x_shape,
            scratch_shapes=[
                pltpu.VMEM((N_FF_TILES, 2 * FF_TILE // LANES, SUBLANES, LANES), F32),
                pltpu.VMEM((FFN_TC, D_MODEL), BF16),
                pltpu.VMEM((2 * FF_TILE // LANES, SUBLANES + FFN_TC, LANES), F32),
                pltpu.VMEM((2 * FF_TILE // LANES, SUBLANES + FFN_TC, LANES), F32),
                pltpu.VMEM((FFN_TC, FF_TILE), BF16),
                pltpu.VMEM((FFN_TC, FF_TILE), BF16),
                pltpu.VMEM((FFN_TC, D_MODEL), F32),
            ],
            compiler_params=pltpu.CompilerParams(dimension_semantics=("arbitrary", "arbitrary"),
                                                 vmem_limit_bytes=VMEM_LIMIT),
            name="convffn",
        )(x, vec(g_ffn_pre), w_up_b, cw_t, cb_t, w_dn_t, vec(g_ffn_post))
    return x
```

```python
import functools
import math

import jax
import jax.numpy as jnp
from jax import lax
from jax.experimental import pallas as pl
from jax.experimental.pallas import tpu as pltpu

F32 = jnp.float32
BF16 = jnp.bfloat16

D_MODEL = 1024
D_SSM = 512
SSM_GROUP = 16
N_SSM_GROUPS = 32
SSM_STATE = 64
D_GMLP = 512
GMLP_HEADS = 8
GMLP_HEAD_DIM = 64
CHUNK = 128
MEM_LEN = 256
XATTN_HEADS = 4
XATTN_HEAD_DIM = 256
D_FF = 2816
RMS_EPS = 1e-6
GELU_K0 = math.sqrt(2.0 / math.pi)
GELU_K1 = GELU_K0 * 0.044715

LANES = 128
SUBLANES = 8
MXU_DIM = 256
N_SLABS = 2 * N_SSM_GROUPS * SSM_STATE // LANES
VMEM_LIMIT = 40 * 1024 * 1024

MIX_TC = 256
ATT_TC = 1024
ATT_SPLIT = 2
ATT_STRIP = 16
FFN_TC = 512
FF_TILE = MXU_DIM
FFN_STRIP = 16
N_FF_TILES = D_FF // FF_TILE


def _rmsnorm(xf, g):
    ms = jnp.mean(xf * xf, axis=-1, keepdims=True)
    return xf * lax.rsqrt(ms + RMS_EPS) * g


def _dot(a, b):
    return jnp.dot(a, b, preferred_element_type=F32)


def _const_spec(shape, layer):
    nd = len(shape)
    return pl.BlockSpec((None,) + tuple(shape), lambda *_: (layer,) + (0,) * nd,
                        pipeline_mode=pl.Buffered(1))


def _zoh(lr, li, log_dt):
    dt = jnp.exp(log_dt)
    mag = jnp.exp(lr * dt)
    ab_r = mag * jnp.cos(li * dt)
    ab_i = mag * jnp.sin(li * dt)
    nr, ni = ab_r - 1.0, ab_i
    den = lr * lr + li * li
    fr = (nr * lr + ni * li) / den
    fi = (ni * lr - nr * li) / den
    return ab_r, ab_i, fr, fi


def _prep_kernel(lr_ref, li_ref, ld_ref, br_ref, bi_ref, ar_ref, ai_ref, bbr_ref, bbi_ref):
    depth, groups = lr_ref.shape[0], lr_ref.shape[1]
    for l in range(depth):
        ab_r, ab_i, fr, fi = _zoh(lr_ref[l], li_ref[l], ld_ref[l])
        ar_ref[l] = ab_r
        ai_ref[l] = ab_i
        for g in range(groups):
            br, bi = br_ref[l, g], bi_ref[l, g]
            bbr_ref[l, g] = fr[g:g + 1, :] * br - fi[g:g + 1, :] * bi
            bbi_ref[l, g] = fr[g:g + 1, :] * bi + fi[g:g + 1, :] * br


def _kv_kernel(mem_ref, g_ref, wk_ref, wv_ref, kt_ref, v_ref):
    mn = _rmsnorm(mem_ref[...], g_ref[...]).astype(BF16)
    k = _dot(mn, wk_ref[...].astype(BF16))
    kt_ref[...] = k.T.astype(BF16)
    v_ref[...] = _dot(mn, wv_ref[...].astype(BF16)).astype(BF16)


def _mixer_kernel(x_ref, gpre_ref, win_ref, wb_ref, ar_ref, ai_ref, wcr_ref, wci_ref, dskip_ref,
                  gv_ref, ones_ref, wsp_ref, bias_ref, wout_ref, gpost_ref,
                  o_ref, st_ref, drv_ref, *, nb, tc):
    m = nb * tc
    nblk = tc // SUBLANES

    @pl.when(pl.program_id(0) == 0)
    def _():
        st_ref[...] = jnp.zeros_like(st_ref)

    x = x_ref[...].reshape(m, D_MODEL)
    h = _rmsnorm(x, gpre_ref[...]).astype(BF16)
    proj = _dot(h, win_ref[...])
    s_in = proj[:, 0:D_SSM]
    s_gate = proj[:, D_SSM:2 * D_SSM]
    g_u = proj[:, 2 * D_SSM:2 * D_SSM + D_GMLP]
    g_vv = proj[:, 2 * D_SSM + D_GMLP:]

    ssum = _dot((g_vv * g_vv).astype(BF16), ones_ref[...])
    vn = g_vv * lax.rsqrt(ssum * (1.0 / GMLP_HEAD_DIM) + RMS_EPS) * gv_ref[...]
    lane = lax.broadcasted_iota(jnp.int32, (m, D_GMLP), 1) % LANES
    vn_lo = jnp.where(lane < GMLP_HEAD_DIM, vn, 0.0).astype(BF16)
    vn_hi = jnp.where(lane >= GMLP_HEAD_DIM, vn, 0.0).astype(BF16)
    rowi = lax.broadcasted_iota(jnp.int32, (CHUNK, 2 * CHUNK), 0)
    coli = lax.broadcasted_iota(jnp.int32, (CHUNK, 2 * CHUNK), 1) % CHUNK
    wsm = [jnp.where(rowi >= coli, wsp_ref[k], 0.0).astype(BF16) for k in range(GMLP_HEADS // 2)]
    chunks = []
    for c in range(m // CHUNK):
        rs = slice(CHUNK * c, CHUNK * (c + 1))
        outs = []
        for k in range(GMLP_HEADS // 2):
            cs = slice(LANES * k, LANES * (k + 1))
            rhs = jnp.concatenate([vn_lo[rs, cs], vn_hi[rs, cs]], axis=0)
            outs.append(_dot(wsm[k], rhs))
        chunks.append(jnp.concatenate(outs, axis=1) + bias_ref[...])
    y_gmlp = g_u * jnp.concatenate(chunks, axis=0)

    s_in_bf = s_in.astype(BF16)
    n_tiles = N_SLABS * LANES // MXU_DIM
    for j in range(n_tiles):
        ks = LANES * ((j % (n_tiles // 2)) // 2)
        bt = _dot(s_in_bf[:, ks:ks + LANES], wb_ref[ks:ks + LANES, MXU_DIM * j:MXU_DIM * (j + 1)])
        for k in range(2):
            s = 2 * j + k
            drv_ref[:, SUBLANES * s:SUBLANES * (s + 1), :] = (
                bt[:, LANES * k:LANES * (k + 1)].reshape(m // SUBLANES, SUBLANES, LANES))

    ar = (ar_ref[0:8, :], ar_ref[8:16, :])
    ai = (ai_ref[0:8, :], ai_ref[8:16, :])
    half = N_SLABS // 2 * SUBLANES

    def scan_block(tb, carry):
        carry = list(carry)
        for r in range(SUBLANES):
            for b in range(nb):
                row = b * nblk + tb
                for p in range(2):
                    xr, xi = carry[4 * b + p], carry[4 * b + 2 + p]
                    rr = pl.ds(64 * p + r, SUBLANES, stride=SUBLANES)
                    ri = pl.ds(half + 64 * p + r, SUBLANES, stride=SUBLANES)
                    nr = ar[p] * xr - ai[p] * xi + drv_ref[row, rr, :]
                    ni = ar[p] * xi + ai[p] * xr + drv_ref[row, ri, :]
                    drv_ref[row, rr, :] = nr
                    drv_ref[row, ri, :] = ni
                    carry[4 * b + p], carry[4 * b + 2 + p] = nr, ni
        return tuple(carry)

    init = tuple(st_ref[SUBLANES * i:SUBLANES * (i + 1), :] for i in range(4 * nb))
    fin = lax.fori_loop(0, nblk, scan_block, init)
    for i in range(4 * nb):
        st_ref[SUBLANES * i:SUBLANES * (i + 1), :] = fin[i]

    def slab_pair(s):
        parts = [drv_ref[:, SUBLANES * q:SUBLANES * (q + 1), :].reshape(m, LANES) for q in (s, s + 1)]
        return jnp.concatenate(parts, axis=1).astype(BF16)

    ys = []
    for n in range(D_SSM // MXU_DIM):
        accr = acci = None
        for q in range(4):
            sr = 8 * n + 2 * q
            rows = slice(LANES * sr, LANES * sr + MXU_DIM)
            cols = slice(MXU_DIM * n, MXU_DIM * (n + 1))
            dr = _dot(slab_pair(sr), wcr_ref[rows, cols])
            di = _dot(slab_pair(N_SLABS // 2 + sr), wci_ref[rows, cols])
            accr = dr if accr is None else accr + dr
            acci = di if acci is None else acci + di
        ys.append(accr - acci)
    y = jnp.concatenate(ys, axis=1) + dskip_ref[...] * s_in
    y_ssm = jax.nn.gelu(y) * jax.nn.sigmoid(s_gate)

    mix_in = jnp.concatenate([y_ssm, y_gmlp], axis=1).astype(BF16)
    mix = _dot(mix_in, wout_ref[...])
    o_ref[...] = (x + _rmsnorm(mix, gpost_ref[...])).reshape(nb, tc, D_MODEL)


def _xattn_kernel(x_ref, gpre_ref, wq32_ref, kt_ref, v_ref, wo32_ref, gpost_ref, o_ref,
                  wq_ref, wo_ref, *scratch, n_split):
    h_ref, q_ref, att_ref, xa_ref = (scratch[i * n_split:(i + 1) * n_split] for i in range(4))

    @pl.when((pl.program_id(0) == 0) & (pl.program_id(1) == 0))
    def _():
        wq_ref[...] = wq32_ref[...].astype(BF16)
        wo_ref[...] = wo32_ref[...].astype(BF16)

    rows = x_ref.shape[0] // n_split

    strips = range(0, rows, ATT_STRIP)

    def norm_in(part):
        for r0 in strips:
            x = x_ref[pl.ds(part * rows + r0, ATT_STRIP), :]
            h_ref[part][pl.ds(r0, ATT_STRIP), :] = _rmsnorm(x, gpre_ref[...]).astype(BF16)

    def query(part):
        q_ref[part][...] = (_dot(h_ref[part][...], wq_ref[...]) * (1.0 / math.sqrt(XATTN_HEAD_DIM))).astype(BF16)

    def attend(part):
        for hd in range(XATTN_HEADS):
            cs = slice(XATTN_HEAD_DIM * hd, XATTN_HEAD_DIM * (hd + 1))
            s = _dot(q_ref[part][:, cs], kt_ref[cs, :])
            ps = []
            for r0 in strips:
                ss = s[r0:r0 + ATT_STRIP, :]
                e = jnp.exp(ss - jnp.max(ss, axis=-1, keepdims=True))
                ps.append((e / jnp.sum(e, axis=-1, keepdims=True)).astype(BF16))
            att_ref[part][:, cs] = _dot(jnp.concatenate(ps, axis=0), v_ref[:, cs]).astype(BF16)

    def project(part):
        xa_ref[part][...] = _dot(att_ref[part][...], wo_ref[...])

    def finish(part):
        for r0 in strips:
            rs = pl.ds(part * rows + r0, ATT_STRIP)
            o_ref[rs, :] = x_ref[rs, :] + _rmsnorm(xa_ref[part][pl.ds(r0, ATT_STRIP), :], gpost_ref[...])

    norm_in(0)
    query(0)
    for part in range(n_split):
        if part + 1 < n_split:
            norm_in(part + 1)
            query(part + 1)
        attend(part)
        project(part)
        if part >= 1:
            finish(part - 1)
    finish(n_split - 1)


def _ffn_kernel(x_ref, gpre_ref, wup_ref, cw_ref, cb_ref, wdn_ref, gpost_ref, o_ref,
                carry_ref, h_ref, a0_ref, a1_ref, act0_ref, act1_ref, acc_ref, *, tc):
    @pl.when(pl.program_id(1) == 0)
    def _():
        carry_ref[...] = jnp.zeros_like(carry_ref)

    h_ref[...] = _rmsnorm(x_ref[...], gpre_ref[...]).astype(BF16)
    acc_ref[...] = jnp.zeros_like(acc_ref)
    pad = SUBLANES
    half = FF_TILE // LANES

    def up(f, a_ref):
        for part in range(2):
            col = pl.multiple_of(part * D_FF + f * FF_TILE, FF_TILE)
            res = _dot(h_ref[...], wup_ref[:, pl.ds(col, FF_TILE)])
            for kk in range(half):
                k = part * half + kk
                a_ref[k, 0:pad, :] = carry_ref[f, k]
                a_ref[k, pad:pad + tc, :] = res[:, kk * LANES:(kk + 1) * LANES]
                carry_ref[f, k] = a_ref[k, tc:tc + pad, :]

    def conv_act(f, a_ref, act_ref):
        for kk in range(half):
            slabs = (kk, half + kk)
            scale = (0.5, 1.0)
            taps = [[s * jnp.broadcast_to(cw_ref[f, j:j + 1, k * LANES:(k + 1) * LANES], (SUBLANES, LANES))
                     for j in range(3)] for k, s in zip(slabs, scale)]
            bias = [s * jnp.broadcast_to(cb_ref[f, :, k * LANES:(k + 1) * LANES], (SUBLANES, LANES))
                    for k, s in zip(slabs, scale)]
            for r0 in range(0, tc, FFN_STRIP):
                cs = [[], []]
                for r in range(r0, r0 + FFN_STRIP, SUBLANES):
                    for i, k in enumerate(slabs):
                        a2 = a_ref[k, pl.ds(pad + r - 2, SUBLANES), :]
                        a1 = a_ref[k, pl.ds(pad + r - 1, SUBLANES), :]
                        a0 = a_ref[k, pl.ds(pad + r, SUBLANES), :]
                        cs[i].append(taps[i][0] * a2 + taps[i][1] * a1 + taps[i][2] * a0 + bias[i])
                val = jnp.concatenate(cs[0], axis=0)
                gate = jnp.concatenate(cs[1], axis=0)
                th = jnp.tanh(gate * (GELU_K0 + GELU_K1 * (gate * gate)))
                act_ref[pl.ds(r0, FFN_STRIP), kk * LANES:(kk + 1) * LANES] = (
                    (val * gate) * (th + 1.0)).astype(BF16)

    def down(f, act_ref):
        acc_ref[...] += _dot(act_ref[...], wdn_ref[f])

    def step(f, a_cur, act_cur, a_nxt, act_prv):
        up(f + 1, a_nxt)
        down(f - 1, act_prv)
        conv_act(f, a_cur, act_cur)

    up(0, a0_ref)
    up(1, a1_ref)
    conv_act(0, a0_ref, act0_ref)

    def pair(k, c):
        f = 2 * k + 1
        step(f, a1_ref, act1_ref, a0_ref, act0_ref)
        step(f + 1, a0_ref, act0_ref, a1_ref, act1_ref)
        return c

    n_pairs = (N_FF_TILES - 2) // 2
    lax.fori_loop(0, n_pairs, pair, 0)
    step(N_FF_TILES - 2, a1_ref, act1_ref, a0_ref, act0_ref)
    conv_act(N_FF_TILES - 1, a0_ref, act0_ref)
    down(N_FF_TILES - 2, act1_ref)
    down(N_FF_TILES - 1, act0_ref)
    o_ref[...] = x_ref[...] + _rmsnorm(acc_ref[...], gpost_ref[...])


def kernel(x, mem, g_mix_pre, w_in, lam_re, lam_im, log_dt, b_re, b_im, c_re, c_im, d_skip, g_v, w_s, b_s, w_out, g_mix_post, g_x_pre, g_mem, w_q, w_k, w_v, w_o, g_x_post, g_ffn_pre, w_up, conv_w, conv_b, w_down, g_ffn_post):
    bsz, seq, d = x.shape
    depth = w_in.shape[0]
    G, P, C = N_SSM_GROUPS, SSM_STATE, SSM_GROUP
    assert d == D_MODEL and seq % FFN_TC == 0 and seq % MIX_TC == 0 and mem.shape[1] == MEM_LEN
    assert seq % ATT_TC == 0 and ATT_TC % (ATT_SPLIT * ATT_STRIP) == 0
    assert N_FF_TILES % 2 == 1 and N_FF_TILES >= 3

    a_shape = (depth, G * P // LANES, LANES)
    gcp = lambda t: jnp.transpose(t, (0, 1, 3, 2))
    ar, ai, bbr, bbi = pl.pallas_call(
        _prep_kernel,
        out_shape=[jax.ShapeDtypeStruct((depth, G, P), F32)] * 2
        + [jax.ShapeDtypeStruct((depth, G, C, P), F32)] * 2,
        name="ssm_prep",
    )(lam_re, lam_im, log_dt[:, :, None], gcp(b_re), gcp(b_im))
    ar, ai = ar.reshape(a_shape), ai.reshape(a_shape)
    bbr, bbi = bbr.reshape(depth, G * C, P), bbi.reshape(depth, G * C, P)

    def block_diag(rows, n_in, n_out):
        r = lax.broadcasted_iota(jnp.int32, (G * n_in, G * n_out), 0) // n_in
        c = lax.broadcasted_iota(jnp.int32, (G * n_in, G * n_out), 1) // n_out
        return jnp.where(r == c, jnp.tile(rows, (1, 1, G)), 0.0).astype(BF16)

    gpc = lambda t: jnp.transpose(t, (0, 1, 3, 2)).reshape(depth, G * P, C)
    wb = jnp.concatenate([block_diag(bbr, C, P), block_diag(bbi, C, P)], axis=2)
    wcr = block_diag(gpc(c_re), P, C)
    wci = block_diag(gpc(c_im), P, C)
    dsk = d_skip.reshape(depth, 1, D_SSM)

    vec = lambda t: t.reshape(depth, 1, -1)
    ones_bd = jnp.kron(jnp.eye(GMLP_HEADS, dtype=F32), jnp.ones((GMLP_HEAD_DIM, GMLP_HEAD_DIM), F32)).astype(BF16)
    wsp = (w_s.reshape(depth, GMLP_HEADS // 2, 2, CHUNK, CHUNK).transpose(0, 1, 3, 2, 4)
           .reshape(depth, GMLP_HEADS // 2, CHUNK, 2 * CHUNK))
    bias2d = jnp.repeat(jnp.transpose(b_s, (0, 2, 1)), GMLP_HEAD_DIM, axis=2)

    w_in_b, w_out_b = w_in.astype(BF16), w_out.astype(BF16)
    ff_tiles = lambda t: (t.reshape(t.shape[:-1] + (2, N_FF_TILES, FF_TILE))
                          .swapaxes(-3, -2).reshape(t.shape[:-1] + (N_FF_TILES, 2 * FF_TILE)))
    w_up_b = w_up.astype(BF16)
    cw_t = jnp.moveaxis(ff_tiles(conv_w), 2, 1)
    cb_t = ff_tiles(conv_b)[:, :, None, :]
    w_dn_t = w_down.astype(BF16).reshape(depth, N_FF_TILES, FF_TILE, D_MODEL)

    kt, vv = pl.pallas_call(
        _kv_kernel,
        grid=(depth, bsz),
        in_specs=[
            pl.BlockSpec((None, MEM_LEN, D_MODEL), lambda l, b: (b, 0, 0)),
            pl.BlockSpec((None, 1, D_MODEL), lambda l, b: (l, 0, 0)),
            pl.BlockSpec((None, D_MODEL, D_MODEL), lambda l, b: (l, 0, 0)),
            pl.BlockSpec((None, D_MODEL, D_MODEL), lambda l, b: (l, 0, 0)),
        ],
        out_specs=[
            pl.BlockSpec((None, None, D_MODEL, MEM_LEN), lambda l, b: (l, b, 0, 0)),
            pl.BlockSpec((None, None, MEM_LEN, D_MODEL), lambda l, b: (l, b, 0, 0)),
        ],
        out_shape=[jax.ShapeDtypeStruct((depth, bsz, D_MODEL, MEM_LEN), BF16),
                   jax.ShapeDtypeStruct((depth, bsz, MEM_LEN, D_MODEL), BF16)],
        compiler_params=pltpu.CompilerParams(dimension_semantics=("arbitrary", "arbitrary"),
                                             vmem_limit_bytes=VMEM_LIMIT),
        name="mem_kv",
    )(mem, vec(g_mem), w_k, w_v)

    x_shape = jax.ShapeDtypeStruct(x.shape, F32)
    for l in range(depth):
        cs = functools.partial(_const_spec, layer=l)
        x = pl.pallas_call(
            functools.partial(_mixer_kernel, nb=bsz, tc=MIX_TC),
            grid=(seq // MIX_TC,),
            in_specs=[
                pl.BlockSpec((bsz, MIX_TC, D_MODEL), lambda t: (0, t, 0)),
                cs((1, D_MODEL)), cs((D_MODEL, 2 * D_SSM + 2 * D_GMLP)), cs((D_SSM, N_SLABS * LANES)),
                cs(a_shape[1:]), cs(a_shape[1:]), cs((G * P, D_SSM)), cs((G * P, D_SSM)), cs((1, D_SSM)),
                cs((1, D_GMLP)),
                pl.BlockSpec((D_GMLP, D_GMLP), lambda t: (0, 0), pipeline_mode=pl.Buffered(1)),
                cs((GMLP_HEADS // 2, CHUNK, 2 * CHUNK)), cs((CHUNK, D_GMLP)),
                cs((D_MODEL, D_MODEL)), cs((1, D_MODEL)),
            ],
            out_specs=pl.BlockSpec((bsz, MIX_TC, D_MODEL), lambda t: (0, t, 0)),
            out_shape=x_shape,
            scratch_shapes=[
                pltpu.VMEM((4 * bsz * SUBLANES, LANES), F32),
                pltpu.VMEM((bsz * MIX_TC // SUBLANES, N_SLABS * SUBLANES, LANES), F32),
            ],
            compiler_params=pltpu.CompilerParams(dimension_semantics=("arbitrary",),
                                                 vmem_limit_bytes=VMEM_LIMIT),
            name="mixer",
        )(x, vec(g_mix_pre), w_in_b, wb, ar, ai, wcr, wci, dsk, vec(g_v), ones_bd, wsp, bias2d,
          w_out_b, vec(g_mix_post))

        x = pl.pallas_call(
            functools.partial(_xattn_kernel, n_split=ATT_SPLIT),
            grid=(bsz, seq // ATT_TC),
            in_specs=[
                pl.BlockSpec((None, ATT_TC, D_MODEL), lambda b, t: (b, t, 0)),
                cs((1, D_MODEL)), cs((D_MODEL, D_MODEL)),
                pl.BlockSpec((None, None, D_MODEL, MEM_LEN), lambda b, t: (l, b, 0, 0)),
                pl.BlockSpec((None, None, MEM_LEN, D_MODEL), lambda b, t: (l, b, 0, 0)),
                cs((D_MODEL, D_MODEL)), cs((1, D_MODEL)),
            ],
            out_specs=pl.BlockSpec((None, ATT_TC, D_MODEL), lambda b, t: (b, t, 0)),
            out_shape=x_shape,
            scratch_shapes=[pltpu.VMEM((D_MODEL, D_MODEL), BF16)] * 2
            + [pltpu.VMEM((ATT_TC // ATT_SPLIT, D_MODEL), dt)
               for dt in (BF16, BF16, BF16, F32) for _ in range(ATT_SPLIT)],
            compiler_params=pltpu.CompilerParams(dimension_semantics=("arbitrary", "arbitrary"),
                                                 vmem_limit_bytes=VMEM_LIMIT),
            name="xattn",
        )(x, vec(g_x_pre), w_q, kt, vv, w_o, vec(g_x_post))

        x = pl.pallas_call(
            functools.partial(_ffn_kernel, tc=FFN_TC),
            grid=(bsz, seq // FFN_TC),
            in_specs=[
                pl.BlockSpec((None, FFN_TC, D_MODEL), lambda b, t: (b, t, 0)),
                cs((1, D_MODEL)), cs((D_MODEL, 2 * D_FF)), cs((N_FF_TILES, 3, 2 * FF_TILE)),
                cs((N_FF_TILES, 1, 2 * FF_TILE)), cs((N_FF_TILES, FF_TILE, D_MODEL)), cs((1, D_MODEL)),
            ],
            out_specs=pl.BlockSpec((None, FFN_TC, D_MODEL), lambda b, t: (b, t, 0)),
            out_shape=x_shape,
            scratch_shapes=[
                pltpu.VMEM((N_FF_TILES, 2 * FF_TILE // LANES, SUBLANES, LANES), F32),
                pltpu.VMEM((FFN_TC, D_MODEL), BF16),
                pltpu.VMEM((2 * FF_TILE // LANES, SUBLANES + FFN_TC, LANES), F32),
                pltpu.VMEM((2 * FF_TILE // LANES, SUBLANES + FFN_TC, LANES), F32),
                pltpu.VMEM((FFN_TC, FF_TILE), BF16),
                pltpu.VMEM((FFN_TC, FF_TILE), BF16),
                pltpu.VMEM((FFN_TC, D_MODEL), F32),
            ],
            compiler_params=pltpu.CompilerParams(dimension_semantics=("arbitrary", "arbitrary"),
                                                 vmem_limit_bytes=VMEM_LIMIT),
            name="convffn",
        )(x, vec(g_ffn_pre), w_up_b, cw_t, cb_t, w_dn_t, vec(g_ffn_post))
    return x
```
